```python
import math
import jax, jax.numpy as jnp
from jax import lax
import numpy as np

D_MODEL = 2048
BATCH = 16
SEQ = 2048
DEPTH = 2

GRID_W = 64
CTX_LEN = 256
N_HEADS = 8
HEAD_DIM = 128
V_DIM = 2 * HEAD_DIM
QK_W = N_HEADS * 2 * HEAD_DIM
ATTN_W = N_HEADS * V_DIM
CONV_W = D_MODEL
CONV_K = 3
ROPE_BASE = 10000.0
ROPE_FREQS = HEAD_DIM // 4
Q_BLOCK = 128
N_EXPERTS = 32
TOP_K = 4
D_EXPERT = D_MODEL // 2
SWIGLU_LIMIT = 7.0
SWIGLU_ALPHA = 1.702
N_MOD = 6
RMS_EPS = 1e-6
ATTN_SCALE = HEAD_DIM ** -0.5
SPLITS = (QK_W, 2 * QK_W, 2 * QK_W + ATTN_W, 2 * QK_W + ATTN_W + CONV_W,
          2 * QK_W + ATTN_W + 2 * CONV_W, 2 * QK_W + ATTN_W + 3 * CONV_W,
          2 * QK_W + ATTN_W + 3 * CONV_W + D_MODEL)
IN_COLS = SPLITS[-1] + D_MODEL

kernel_name = "hybrid_diffattn_shortconv_moe_dit"


def rms_norm(x, g):
    xf = x.astype(jnp.float32)
    xf = xf * lax.rsqrt(jnp.mean(xf * xf, axis=-1, keepdims=True) + RMS_EPS)
    return xf.astype(x.dtype) * g


def adaln(cv, w, b):
    m = jax.nn.silu(cv) @ w + b
    return jnp.split(m[:, None, :], N_MOD, axis=-1)


def modulate(x, g, shift, scale):
    return rms_norm(x, g) * (1 + scale) + shift


def axial_rope(rows, dtype):
    n_tok = rows * GRID_W
    row = jnp.repeat(jnp.arange(rows, dtype=jnp.float32), GRID_W)
    col = (jnp.arange(n_tok) % GRID_W).astype(jnp.float32)
    inv = ROPE_BASE ** (-jnp.arange(ROPE_FREQS, dtype=jnp.float32) / ROPE_FREQS)
    ang = jnp.stack([row[:, None] * inv, col[:, None] * inv], axis=1)
    ang = jnp.stack([ang, ang], axis=2).reshape(n_tok, HEAD_DIM)
    return jnp.cos(ang).astype(dtype), jnp.sin(ang).astype(dtype)


def apply_rope(x, cos, sin):
    xs = x.reshape(*x.shape[:-1], 2, 2, ROPE_FREQS)
    rot = jnp.stack([-xs[..., 1, :], xs[..., 0, :]], axis=-2).reshape(x.shape)
    return x * cos[None, :, None, None, :] + rot * sin[None, :, None, None, :]


def qk_heads(z, gain):
    z = z.reshape(*z.shape[:-1], N_HEADS, 2, HEAD_DIM)
    return rms_norm(z, gain)


def diff_lambda(lam_qk, lam_init):
    lq = lam_qk.astype(jnp.float32)
    return jnp.exp(jnp.sum(lq[0] * lq[1])) - jnp.exp(jnp.sum(lq[2] * lq[3])) + lam_init


def diff_attention(q, k, v, lam, lam_init, subln_g):
    s = jnp.einsum('bqhmd,bkhmd->bhmqk', q, k).astype(jnp.float32) * ATTN_SCALE
    p = jax.nn.softmax(s, axis=-1)
    a = (p[:, :, 0] - lam * p[:, :, 1]).astype(v.dtype)
    o = jnp.einsum('bhqk,bkhe->bqhe', a, v)
    o = rms_norm(o, subln_g) * (1.0 - lam_init)
    return o.reshape(o.shape[0], o.shape[1], ATTN_W)


def short_conv(u, b_gate, c_gate, w):
    y = lax.conv_general_dilated(c_gate * u, w[:, None, :], window_strides=(1,),
                                 padding=((CONV_K // 2, CONV_K // 2),),
                                 dimension_numbers=('NWC', 'WIO', 'NWC'),
                                 feature_group_count=CONV_W)
    return b_gate * y


def merge_branches(y_attn, y_conv, g_attn, g_conv, w_attn_out, w_conv_out, w_mix_out):
    y = jax.nn.sigmoid(g_attn) * (y_attn @ w_attn_out) + jax.nn.sigmoid(g_conv) * (y_conv @ w_conv_out)
    return y @ w_mix_out


def moe_ffn(h, w_router, b_router, w_gate_up, b_gate_up, w_down, b_down):
    logits = (h @ w_router + b_router).astype(jnp.float32)
    top_val, top_idx = lax.top_k(logits, TOP_K)
    top_w = jax.nn.softmax(top_val, axis=-1)
    gates = jnp.einsum('...k,...ke->...e', top_w,
                       jax.nn.one_hot(top_idx, N_EXPERTS, dtype=jnp.float32)).astype(h.dtype)
    out = jnp.zeros_like(h)
    for e in range(N_EXPERTS):
        gu = h @ w_gate_up[e] + b_gate_up[e]
        gate, lin = jnp.split(gu, 2, axis=-1)
        gate = jnp.minimum(gate, SWIGLU_LIMIT)
        lin = jnp.clip(lin, -SWIGLU_LIMIT, SWIGLU_LIMIT)
        act = gate * jax.nn.sigmoid(SWIGLU_ALPHA * gate) * (lin + 1)
        out = out + gates[..., e:e + 1] * (act @ w_down[e] + b_down[e])
    return out


def setup_inputs(seed: int = 0) -> dict:
    key = jax.random.key(seed)
    ks = jax.random.split(key, 23)
    D, L, E, F = D_MODEL, DEPTH, N_EXPERTS, D_EXPERT

    def nrm(k, shape, s):
        return jax.random.normal(k, shape, jnp.float32) * s

    return {
        "x": nrm(ks[0], (BATCH, SEQ, D), 1.0),
        "c": nrm(ks[1], (BATCH, D), 1.0),
        "ctx": nrm(ks[2], (BATCH, CTX_LEN, D), 1.0),
        "c_ctx": nrm(ks[3], (D,), 1.0),
        "w_ada": nrm(ks[4], (L, D, N_MOD * D), 0.5 * D ** -0.5),
        "b_ada": nrm(ks[5], (L, N_MOD * D), 0.02),
        "norm_mix_g": 1.0 + nrm(ks[6], (L, D), 0.02),
        "norm_ffn_g": 1.0 + nrm(ks[7], (L, D), 0.02),
        "w_in": nrm(ks[8], (L, D, IN_COLS), D ** -0.5),
        "q_norm_g": 1.0 + nrm(ks[9], (L, HEAD_DIM), 0.02),
        "k_norm_g": 1.0 + nrm(ks[10], (L, HEAD_DIM), 0.02),
        "lambda_qk": nrm(ks[11], (L, 4, HEAD_DIM), 0.1),
        "subln_g": 1.0 + nrm(ks[12], (L, V_DIM), 0.02),
        "conv_w": nrm(ks[13], (L, CONV_K, CONV_W), CONV_K ** -0.5),
        "w_attn_out": nrm(ks[14], (L, ATTN_W, D), ATTN_W ** -0.5),
        "w_conv_out": nrm(ks[15], (L, CONV_W, D), CONV_W ** -0.5),
        "w_mix_out": nrm(ks[16], (L, D, D), D ** -0.5),
        "w_router": nrm(ks[17], (L, D, E), D ** -0.5),
        "b_router": nrm(ks[18], (L, E), 0.01),
        "w_gate_up": nrm(ks[19], (L, E, D, 2 * F), D ** -0.5),
        "b_gate_up": nrm(ks[20], (L, E, 2 * F), 0.02),
        "w_down": nrm(ks[21], (L, E, F, D), F ** -0.5),
        "b_down": nrm(ks[22], (L, E, D), 0.02),
    }


def reference(x, c, ctx, c_ctx, w_ada, b_ada, norm_mix_g, norm_ffn_g, w_in, q_norm_g, k_norm_g,
              lambda_qk, subln_g, conv_w, w_attn_out, w_conv_out, w_mix_out, w_router, b_router,
              w_gate_up, b_gate_up, w_down, b_down):
    B, S, _ = x.shape
    C = ctx.shape[1]
    ROWS = S // GRID_W
    cos, sin = axial_rope(ROWS, x.dtype)
    n_blk = S // Q_BLOCK

    for i in range(DEPTH):
        last = i == DEPTH - 1
        lam_init = 0.8 - 0.6 * math.exp(-0.3 * i)
        lam = diff_lambda(lambda_qk[i], lam_init)
        sh1, sc1, g1, sh2, sc2, g2 = adaln(c, w_ada[i], b_ada[i])
        csh1, csc1, cg1, csh2, csc2, cg2 = adaln(c_ctx[None], w_ada[i], b_ada[i])

        h_lat = modulate(x, norm_mix_g[i], sh1, sc1)
        h_ctx = modulate(ctx, norm_mix_g[i], csh1, csc1)
        q_l, k_l, v_l, u_l, b_l, c_l, ga_l, gb_l = jnp.split(h_lat @ w_in[i], SPLITS, axis=-1)
        q_l = apply_rope(qk_heads(q_l, q_norm_g[i]), cos, sin)
        k_l = apply_rope(qk_heads(k_l, k_norm_g[i]), cos, sin)
        if last:
            k_c, v_c = jnp.split(h_ctx @ w_in[i][:, SPLITS[0]:SPLITS[2]], [QK_W], axis=-1)
        else:
            q_c, k_c, v_c, u_c, b_c, c_c, ga_c, gb_c = jnp.split(h_ctx @ w_in[i], SPLITS, axis=-1)
        k_c = qk_heads(k_c, k_norm_g[i])
        k_all = jnp.concatenate([k_c, k_l], axis=1)
        v_all = jnp.concatenate([v_c, v_l], axis=1).reshape(B, C + S, N_HEADS, V_DIM)

        sub_g = subln_g[i]
        q_blocks = q_l.reshape(B, n_blk, Q_BLOCK, N_HEADS, 2, HEAD_DIM).swapaxes(0, 1)
        y_attn = lax.map(lambda qb: diff_attention(qb, k_all, v_all, lam, lam_init, sub_g), q_blocks)
        y_attn = y_attn.swapaxes(0, 1).reshape(B, S, ATTN_W)
        y_conv = short_conv(u_l, b_l, c_l, conv_w[i])
        mix_lat = merge_branches(y_attn, y_conv, ga_l, gb_l, w_attn_out[i], w_conv_out[i], w_mix_out[i])

        if not last:
            q_c = qk_heads(q_c, q_norm_g[i])
            y_attn_c = diff_attention(q_c, k_c, v_c.reshape(B, C, N_HEADS, V_DIM), lam, lam_init, sub_g)
            y_conv_c = short_conv(u_c, b_c, c_c, conv_w[i])
            mix_ctx = merge_branches(y_attn_c, y_conv_c, ga_c, gb_c, w_attn_out[i], w_conv_out[i], w_mix_out[i])
            ctx = ctx + cg1 * mix_ctx
        x = x + g1 * mix_lat

        h_lat = modulate(x, norm_ffn_g[i], sh2, sc2)
        if last:
            x = x + g2 * moe_ffn(h_lat, w_router[i], b_router[i], w_gate_up[i], b_gate_up[i],
                                 w_down[i], b_down[i])
        else:
            h_ctx = modulate(ctx, norm_ffn_g[i], csh2, csc2)
            y = moe_ffn(jnp.concatenate([h_ctx, h_lat], axis=1), w_router[i], b_router[i],
                        w_gate_up[i], b_gate_up[i], w_down[i], b_down[i])
            ctx = ctx + cg2 * y[:, :C]
            x = x + g2 * y[:, C:]
    return x
```

```python
import functools
import math

import jax
import jax.numpy as jnp
from jax import lax
from jax.experimental import pallas as pl
from jax.experimental.pallas import tpu as pltpu

GRID_W = 64
TOP_K = 4
ROPE_BASE = 10000.0
RMS_EPS = 1e-6
SWIGLU_LIMIT = 7.0
SWIGLU_ALPHA = 1.702
N_MOD = 6
LANES = 128
VMEM_LIMIT_BYTES = 56 * 2 ** 20

F32 = jnp.float32
BF16 = jnp.bfloat16
NT_DIMS = (((1,), (1,)), ((), ()))


def _cparams(*sem):
    return pltpu.CompilerParams(dimension_semantics=sem, vmem_limit_bytes=VMEM_LIMIT_BYTES)


def _tile(n, pref):
    if n <= pref:
        return n
    for t in range(pref, 7, -1):
        if n % t == 0 and t % 8 == 0:
            return t
    return n


def _pack_pairs(x):
    n = x.shape[1] // 2
    hi = lax.bitcast_convert_type(x[:, :n].astype(BF16).astype(F32), jnp.uint32)
    lo = lax.bitcast_convert_type(x[:, n:].astype(BF16).astype(F32), jnp.uint32)
    return hi | (lo >> 16)


def _unpack_pairs_f32(p):
    left = lax.bitcast_convert_type(p & jnp.uint32(0xFFFF0000), F32)
    right = lax.bitcast_convert_type(p << 16, F32)
    return jnp.concatenate([left, right], axis=1)


def _rms(x):
    return x * lax.rsqrt(jnp.mean(x * x, axis=-1, keepdims=True) + RMS_EPS)


def _adaln_kernel(cv_ref, w_ref, b_ref, o_ref):
    cv = cv_ref[...]
    a = (cv * jax.nn.sigmoid(cv)).astype(BF16)
    o_ref[0] = jnp.dot(a, w_ref[0].astype(BF16), preferred_element_type=F32) + b_ref[0]


def _adaln(cv, w_ada, b_ada):
    n_layers, d, n = w_ada.shape
    r = cv.shape[0]
    tn = _tile(n, 1024)
    return pl.pallas_call(
        _adaln_kernel,
        out_shape=jax.ShapeDtypeStruct((n_layers, r, n), F32),
        grid=(n_layers, n // tn),
        in_specs=[pl.BlockSpec((r, d), lambda l, j: (0, 0)),
                  pl.BlockSpec((1, d, tn), lambda l, j: (l, 0, j)),
                  pl.BlockSpec((1, 1, tn), lambda l, j: (l, 0, j))],
        out_specs=pl.BlockSpec((1, r, tn), lambda l, j: (l, 0, j)),
        compiler_params=_cparams("arbitrary", "arbitrary"),
        name="adaln",
    )(cv, w_ada, b_ada.reshape(n_layers, 1, n))


def _inproj_kernel(x_ref, m_ref, g_ref, w_ref, qg_ref, kg_ref, cos_ref, sin_ref, o_ref, h_ref,
                   *, nq, rope, hd, q_scale):
    j = pl.program_id(1)

    @pl.when(j == 0)
    def _():
        h = _rms(x_ref[...]) * g_ref[...] * (1.0 + m_ref[0, 1:2, :]) + m_ref[0, 0:1, :]
        h_ref[...] = h.astype(BF16)

    acc = jnp.dot(h_ref[...], w_ref[...], preferred_element_type=F32)
    tm, tn = acc.shape

    def qk_epilogue(gain):
        first_half = (lax.broadcasted_iota(jnp.int32, (tm, hd), 1) % (hd // 2)) < (hd // 4)
        for g in range(tn // hd):
            n = _rms(acc[:, g * hd:(g + 1) * hd]) * gain
            if rope:
                fwd = pltpu.roll(n, hd - hd // 4, axis=1)
                bwd = pltpu.roll(n, hd // 4, axis=1)
                n = n * cos_ref[...] + jnp.where(first_half, fwd, bwd) * sin_ref[...]
            o_ref[:, g * hd:(g + 1) * hd] = n.astype(BF16)

    @pl.when(j < nq)
    def _():
        qk_epilogue(qg_ref[...] * q_scale)

    @pl.when((j >= nq) & (j < 2 * nq))
    def _():
        qk_epilogue(kg_ref[...])

    @pl.when(j >= 2 * nq)
    def _():
        o_ref[...] = acc.astype(BF16)


def _inproj(x2, mods, mod_row, norm_g, w_bf, q_g, k_g, cos, sin_s, *, qkw, n_cols, seq, rope, hd):
    rows, d = x2.shape
    tm = _tile(seq, 1024)
    tn = _tile(qkw, 1024)
    tiles_per_seq = seq // tm
    kern = functools.partial(_inproj_kernel, nq=qkw // tn, rope=rope, hd=hd, q_scale=float(hd) ** -0.5)
    return pl.pallas_call(
        kern,
        out_shape=jax.ShapeDtypeStruct((rows, n_cols), BF16),
        grid=(rows // tm, n_cols // tn),
        in_specs=[pl.BlockSpec((tm, d), lambda i, j: (i, 0)),
                  pl.BlockSpec((1, N_MOD, d), lambda i, j: (mod_row(i // tiles_per_seq), 0, 0)),
                  pl.BlockSpec((1, d), lambda i, j: (0, 0)),
                  pl.BlockSpec((d, tn), lambda i, j: (0, j)),
                  pl.BlockSpec((1, hd), lambda i, j: (0, 0)),
                  pl.BlockSpec((1, hd), lambda i, j: (0, 0)),
                  pl.BlockSpec((tm, hd), lambda i, j: ((i % tiles_per_seq) if rope else 0, 0)),
                  pl.BlockSpec((tm, hd), lambda i, j: ((i % tiles_per_seq) if rope else 0, 0))],
        out_specs=pl.BlockSpec((tm, tn), lambda i, j: (i, j)),
        scratch_shapes=[pltpu.VMEM((tm, d), BF16)],
        compiler_params=_cparams("arbitrary", "arbitrary"),
        name="inproj",
    )(x2, mods, norm_g, w_bf, q_g, k_g, cos, sin_s)


def _attn_kernel(lq_ref, sg_ref, q_ref, *refs, n_src, lam_init, hd):
    k_refs = refs[0:2 * n_src:2]
    v_refs = refs[1:2 * n_src:2]
    o_ref = refs[2 * n_src]
    lq = lq_ref[...]
    lam = (jnp.exp(jnp.sum(lq[0:1] * lq[1:2], axis=1, keepdims=True))
           - jnp.exp(jnp.sum(lq[2:3] * lq[3:4], axis=1, keepdims=True)) + lam_init)
    q = q_ref[...]
    probs, norms = [], []
    for m in range(2):
        qm = q[:, m * hd:(m + 1) * hd]
        ss = [lax.dot_general(qm, k[:, m * hd:(m + 1) * hd], NT_DIMS, preferred_element_type=F32)
              for k in k_refs]
        mx = functools.reduce(jnp.maximum, [jnp.max(s, axis=1, keepdims=True) for s in ss])
        es = [jnp.exp(s - mx) for s in ss]
        probs.append(es)
        norms.append(functools.reduce(jnp.add, [jnp.sum(e, axis=1, keepdims=True) for e in es]))
    inv0 = 1.0 / norms[0]
    inv1 = lam / norms[1]
    o = None
    for si in range(n_src):
        a = (probs[0][si] * inv0 - probs[1][si] * inv1).astype(BF16)
        part = jnp.dot(a, v_refs[si][...], preferred_element_type=F32)
        o = part if o is None else o + part
    o = _rms(o) * sg_ref[...] * (1.0 - lam_init)
    o_ref[...] = o.astype(BF16)


def _attention(lq, sub_g, q_arr, kv_arrs, *, batch, q_len, kv_lens, n_heads, hd, qkw, lam_init):
    vd = 2 * hd
    tq = _tile(q_len, 512)
    nqb = q_len // tq
    k_off = qkw // vd
    v_off = 2 * qkw // vd
    in_specs = [pl.BlockSpec((4, hd), lambda b, h, i: (0, 0)),
                pl.BlockSpec((1, vd), lambda b, h, i: (0, 0)),
                pl.BlockSpec((tq, vd), lambda b, h, i: (b * nqb + i, h))]
    args = [lq, sub_g, q_arr]
    for arr, n in zip(kv_arrs, kv_lens):
        in_specs.append(pl.BlockSpec((n, vd), lambda b, h, i: (b, k_off + h)))
        in_specs.append(pl.BlockSpec((n, vd), lambda b, h, i: (b, v_off + h)))
        args += [arr, arr]
    kern = functools.partial(_attn_kernel, n_src=len(kv_arrs), lam_init=lam_init, hd=hd)
    return pl.pallas_call(
        kern,
        out_shape=jax.ShapeDtypeStruct((batch * q_len, n_heads * vd), BF16),
        grid=(batch, n_heads, nqb),
        in_specs=in_specs,
        out_specs=pl.BlockSpec((tq, vd), lambda b, h, i: (b * nqb + i, h)),
        compiler_params=_cparams("arbitrary", "arbitrary", "arbitrary"),
        name="diff_attn",
    )(*args)


def _conv_kernel(u_ref, b_ref, c_ref, w_ref, o_ref):
    cu = c_ref[...].astype(F32) * u_ref[...].astype(F32)
    n = cu.shape[0]
    row = lax.broadcasted_iota(jnp.int32, cu.shape, 0)
    prev = jnp.where(row == 0, 0.0, pltpu.roll(cu, 1, axis=0))
    nxt = jnp.where(row == n - 1, 0.0, pltpu.roll(cu, n - 1, axis=0))
    w = w_ref[...]
    y = b_ref[...].astype(F32) * (w[0:1] * prev + w[1:2] * cu + w[2:3] * nxt)
    o_ref[...] = y.astype(BF16)


def _short_conv(p_arr, conv_w, *, batch, seq, u_off, width):
    tc = _tile(width, 512)
    o = u_off // tc
    nb = width // tc
    return pl.pallas_call(
        _conv_kernel,
        out_shape=jax.ShapeDtypeStruct((batch * seq, width), BF16),
        grid=(batch, nb),
        in_specs=[pl.BlockSpec((seq, tc), lambda b, j: (b, o + j)),
                  pl.BlockSpec((seq, tc), lambda b, j: (b, o + nb + j)),
                  pl.BlockSpec((seq, tc), lambda b, j: (b, o + 2 * nb + j)),
                  pl.BlockSpec((conv_w.shape[0], tc), lambda b, j: (0, j))],
        out_specs=pl.BlockSpec((seq, tc), lambda b, j: (b, j)),
        compiler_params=_cparams("arbitrary", "arbitrary"),
        name="short_conv",
    )(p_arr, p_arr, p_arr, conv_w)


def _merge1_kernel(ya_ref, yc_ref, wa_ref, wc_ref, ga_ref, gb_ref, o_ref):
    a = jnp.dot(ya_ref[...], wa_ref[...], preferred_element_type=F32)
    c = jnp.dot(yc_ref[...], wc_ref[...], preferred_element_type=F32)
    y = jax.nn.sigmoid(ga_ref[...].astype(F32)) * a + jax.nn.sigmoid(gb_ref[...].astype(F32)) * c
    o_ref[...] = y.astype(BF16)


def _merge1(y_attn, y_conv, wa_bf, wc_bf, p_arr, *, ga_off):
    rows, ka = y_attn.shape
    kc = y_conv.shape[1]
    d = wa_bf.shape[1]
    tm = _tile(rows, 1024)
    tn = _tile(d, 512)
    o = ga_off // tn
    nb = d // tn
    return pl.pallas_call(
        _merge1_kernel,
        out_shape=jax.ShapeDtypeStruct((rows, d), BF16),
        grid=(rows // tm, nb),
        in_specs=[pl.BlockSpec((tm, ka), lambda i, j: (i, 0)),
                  pl.BlockSpec((tm, kc), lambda i, j: (i, 0)),
                  pl.BlockSpec((ka, tn), lambda i, j: (0, j)),
                  pl.BlockSpec((kc, tn), lambda i, j: (0, j)),
                  pl.BlockSpec((tm, tn), lambda i, j: (i, o + j)),
                  pl.BlockSpec((tm, tn), lambda i, j: (i, o + nb + j))],
        out_specs=pl.BlockSpec((tm, tn), lambda i, j: (i, j)),
        compiler_params=_cparams("arbitrary", "arbitrary"),
        name="merge_branches",
    )(y_attn, y_conv, wa_bf, wc_bf, p_arr, p_arr)


def _split_bf16(x):
    hi = x.astype(BF16)
    return hi, (x - hi.astype(F32)).astype(BF16)


def _merge2_kernel(y_ref, wm_ref, x_ref, m_ref, gn_ref, wr_ref, br_ref, xo_ref, hp_ref, ids_ref, tw_ref):
    mix = jnp.dot(y_ref[...], wm_ref[...], preferred_element_type=F32)
    xn = x_ref[...] + m_ref[0, 2:3, :] * mix
    xo_ref[...] = xn
    h = _rms(xn) * gn_ref[...] * (1.0 + m_ref[0, 4:5, :]) + m_ref[0, 3:4, :]
    hp_ref[...] = _pack_pairs(h)

    h_hi, h_lo = _split_bf16(h)
    w_hi, w_lo = _split_bf16(wr_ref[...])
    logits = (lax.dot_general(w_hi, h_hi, NT_DIMS, preferred_element_type=F32)
              + lax.dot_general(w_hi, h_lo, NT_DIMS, preferred_element_type=F32)
              + lax.dot_general(w_lo, h_hi, NT_DIMS, preferred_element_type=F32)) + br_ref[...]
    n_exp = logits.shape[0]
    expert = lax.broadcasted_iota(jnp.int32, logits.shape, 0)
    vals = []
    for k in range(TOP_K):
        mx = jnp.max(logits, axis=0, keepdims=True)
        idx = jnp.min(jnp.where(logits == mx, expert, n_exp), axis=0, keepdims=True)
        ids_ref[k:k + 1, :] = idx
        vals.append(mx)
        logits = jnp.where(expert == idx, -jnp.inf, logits)
    es = [jnp.exp(v - vals[0]) for v in vals]
    inv = 1.0 / functools.reduce(jnp.add, es)
    for k in range(TOP_K):
        tw_ref[k:k + 1, :] = es[k] * inv


def _merge2(y, wm_bf, x2, mods, mod_row, norm_g, w_router_t, b_router, *, seq):
    rows, d = x2.shape
    n_exp = w_router_t.shape[0]
    tm = _tile(seq, 512)
    tiles_per_seq = seq // tm
    return pl.pallas_call(
        _merge2_kernel,
        out_shape=(jax.ShapeDtypeStruct((rows, d), F32),
                   jax.ShapeDtypeStruct((rows, d // 2), jnp.uint32),
                   jax.ShapeDtypeStruct((TOP_K, rows), jnp.int32),
                   jax.ShapeDtypeStruct((TOP_K, rows), F32)),
        grid=(rows // tm,),
        in_specs=[pl.BlockSpec((tm, d), lambda i: (i, 0)),
                  pl.BlockSpec((d, d), lambda i: (0, 0)),
                  pl.BlockSpec((tm, d), lambda i: (i, 0)),
                  pl.BlockSpec((1, N_MOD, d), lambda i: (mod_row(i // tiles_per_seq), 0, 0)),
                  pl.BlockSpec((1, d), lambda i: (0, 0)),
                  pl.BlockSpec((n_exp, d), lambda i: (0, 0)),
                  pl.BlockSpec((n_exp, 1), lambda i: (0, 0))],
        out_specs=(pl.BlockSpec((tm, d), lambda i: (i, 0)),
                   pl.BlockSpec((tm, d // 2), lambda i: (i, 0)),
                   pl.BlockSpec((TOP_K, tm), lambda i: (0, i)),
                   pl.BlockSpec((TOP_K, tm), lambda i: (0, i))),
        compiler_params=_cparams("arbitrary"),
        name="mixout_router",
    )(y, wm_bf, x2, mods, norm_g, w_router_t, b_router)


def _plan_kernel(ids_ref, dest_ref, te_ref, nv_ref, cnt_ref, run_ref, gs_ref, *, n_exp, tm_e):
    p = pl.program_id(0)
    t = pl.program_id(1)
    ids = ids_ref[...]
    tt = ids.shape[1]
    expert = lax.broadcasted_iota(jnp.int32, (n_exp, tt), 0)
    hit = [ids[k:k + 1, :] == expert for k in range(TOP_K)]
    member = functools.reduce(jnp.add, [h.astype(F32) for h in hit])
    tile_count = jnp.sum(member, axis=1, keepdims=True)

    @pl.when((p == 0) & (t == 0))
    def _():
        cnt_ref[...] = jnp.zeros_like(cnt_ref)

    @pl.when(p == 0)
    def _():
        cnt_ref[...] += tile_count

    @pl.when((p == 1) & (t == 0))
    def _():
        cnt = cnt_ref[...]
        ptiles = jnp.floor((cnt + (tm_e - 1)) * (1.0 / tm_e))
        r = lax.broadcasted_iota(jnp.int32, (n_exp, n_exp), 0)
        cidx = lax.broadcasted_iota(jnp.int32, (n_exp, n_exp), 1)
        lower = (cidx < r).astype(BF16)
        p_hi = jnp.floor(ptiles * (1.0 / 16.0))
        p_lo = ptiles - 16.0 * p_hi
        gstart = (16.0 * jnp.dot(lower, p_hi.astype(BF16), preferred_element_type=F32)
                  + jnp.dot(lower, p_lo.astype(BF16), preferred_element_type=F32))
        gs_ref[...] = gstart
        run_ref[...] = jnp.zeros_like(run_ref)
        ntp = te_ref.shape[1]
        tile = lax.broadcasted_iota(jnp.int32, (n_exp, ntp), 1).astype(F32)
        e2 = lax.broadcasted_iota(jnp.int32, (n_exp, ntp), 0).astype(F32)
        gend = gstart[:, 0:1] + ptiles[:, 0:1]
        te = jnp.sum((gend <= tile).astype(F32), axis=0, keepdims=True)
        used = te < n_exp
        te = jnp.minimum(te, n_exp - 1.0)
        onehot = (e2 == te).astype(F32)
        start_e = jnp.sum(onehot * gstart[:, 0:1], axis=0, keepdims=True)
        cnt_e = jnp.sum(onehot * cnt[:, 0:1], axis=0, keepdims=True)
        nv = jnp.clip(cnt_e - (tile[0:1] - start_e) * tm_e, 0.0, float(tm_e))
        te_ref[...] = te.astype(jnp.int32)
        nv_ref[...] = jnp.where(used, nv, 0.0).astype(jnp.int32)

    @pl.when(p == 1)
    def _():
        rr = lax.broadcasted_iota(jnp.int32, (tt, tt), 0)
        cc = lax.broadcasted_iota(jnp.int32, (tt, tt), 1)
        upper = (rr < cc).astype(BF16)
        before = jnp.dot(member.astype(BF16), upper, preferred_element_type=F32)
        pos = gs_ref[:, 0:1] * tm_e + run_ref[:, 0:1] + before
        for k in range(TOP_K):
            d = jnp.sum(jnp.where(hit[k], pos, 0.0), axis=0, keepdims=True)
            dest_ref[k:k + 1, :] = d.astype(jnp.int32)
        run_ref[...] += tile_count


def _plan(ids, *, n_exp, tm_e, n_tiles_pad):
    n_tok = ids.shape[1]
    tt = _tile(n_tok, 512)
    kern = functools.partial(_plan_kernel, n_exp=n_exp, tm_e=tm_e)
    return pl.pallas_call(
        kern,
        out_shape=(jax.ShapeDtypeStruct((TOP_K, n_tok), jnp.int32),
                   jax.ShapeDtypeStruct((1, n_tiles_pad), jnp.int32),
                   jax.ShapeDtypeStruct((1, n_tiles_pad), jnp.int32)),
        grid=(2, n_tok // tt),
        in_specs=[pl.BlockSpec((TOP_K, tt), lambda p, t: (0, t))],
        out_specs=(pl.BlockSpec((TOP_K, tt), lambda p, t: (0, p * t)),
                   pl.BlockSpec((1, n_tiles_pad), lambda p, t: (0, 0)),
                   pl.BlockSpec((1, n_tiles_pad), lambda p, t: (0, 0))),
        scratch_shapes=[pltpu.VMEM((n_exp, LANES), F32)] * 3,
        compiler_params=_cparams("arbitrary", "arbitrary"),
        name="route_plan",
    )(ids)


def _dest_tiles(dest, tt):
    k, n = dest.shape
    return dest.reshape(k, n // tt, tt).transpose(1, 0, 2)


def _dispatch_body(dest_ref, h_ref, hg_ref, sem):
    tt = h_ref.shape[0]

    def row_copy(t, d):
        return pltpu.make_async_copy(h_ref.at[pl.ds(t, 1)], hg_ref.at[pl.ds(d, 1)], sem)

    def issue(t, carry):
        for k in range(TOP_K):
            row_copy(t, dest_ref[0, k, t]).start()
        return carry

    def drain(t, carry):
        for k in range(TOP_K):
            row_copy(0, 0).wait()
        return carry

    lax.fori_loop(0, tt, issue, 0)
    lax.fori_loop(0, tt, drain, 0)


def _dispatch_first_kernel(dest_ref, h_ref, hg_ref, sem):
    _dispatch_body(dest_ref, h_ref, hg_ref, sem)


def _dispatch_next_kernel(dest_ref, h_ref, hg_in_ref, hg_ref, sem):
    del hg_in_ref
    _dispatch_body(dest_ref, h_ref, hg_ref, sem)


def _dispatch(dest, hp, hg, *, n_rows_pad):
    n_tok, dp = hp.shape
    tt = _tile(n_tok, 256)
    dest_t = _dest_tiles(dest, tt)
    in_specs = [pl.BlockSpec((1, TOP_K, tt), lambda i: (i, 0, 0), memory_space=pltpu.SMEM),
                pl.BlockSpec((tt, dp), lambda i: (i, 0))]
    args = [dest_t, hp]
    aliases = {}
    kern = _dispatch_first_kernel
    if hg is not None:
        in_specs.append(pl.BlockSpec(memory_space=pl.ANY))
        args.append(hg)
        aliases = {2: 0}
        kern = _dispatch_next_kernel
    return pl.pallas_call(
        kern,
        out_shape=jax.ShapeDtypeStruct((n_rows_pad, dp), jnp.uint32),
        grid=(n_tok // tt,),
        in_specs=in_specs,
        out_specs=pl.BlockSpec(memory_space=pl.ANY),
        scratch_shapes=[pltpu.SemaphoreType.DMA(())],
        input_output_aliases=aliases,
        compiler_params=pltpu.CompilerParams(dimension_semantics=("arbitrary",),
                                             vmem_limit_bytes=VMEM_LIMIT_BYTES, has_side_effects=True),
        name="moe_dispatch",
    )(*args)


def _mlp_kernel(te_ref, nv_ref, hg_ref, wgu_ref, bgu_ref, wd_ref, bd_ref, o_ref):
    del te_ref
    nv = nv_ref[pl.program_id(0)]

    @pl.when(nv == 0)
    def _():
        o_ref[...] = jnp.zeros_like(o_ref)

    @pl.when(nv > 0)
    def _():
        f = wd_ref.shape[1]
        packed = hg_ref[...]
        row = lax.broadcasted_iota(jnp.int32, packed.shape, 0)
        packed = jnp.where(row < nv, packed, jnp.uint32(0))
        h = _unpack_pairs_f32(packed).astype(BF16)
        gu = jnp.dot(h, wgu_ref[0], preferred_element_type=F32) + bgu_ref[0]
        gate = jnp.minimum(gu[:, :f], SWIGLU_LIMIT)
        lin = jnp.clip(gu[:, f:], -SWIGLU_LIMIT, SWIGLU_LIMIT)
        act = gate * jax.nn.sigmoid(SWIGLU_ALPHA * gate) * (lin + 1.0)
        eo = jnp.dot(act.astype(BF16), wd_ref[0], preferred_element_type=F32) + bd_ref[0]
        o_ref[...] = _pack_pairs(eo)


def _expert_mlp(te, nv, hg, wgu_bf, bgu, wd_bf, bd, *, tm_e, n_tiles):
    n_exp, d, f2 = wgu_bf.shape
    f = wd_bf.shape[1]
    dp = hg.shape[1]
    grid_spec = pltpu.PrefetchScalarGridSpec(
        num_scalar_prefetch=2,
        grid=(n_tiles,),
        in_specs=[pl.BlockSpec((tm_e, dp), lambda i, te, nv: (i, 0)),
                  pl.BlockSpec((1, d, f2), lambda i, te, nv: (te[i], 0, 0)),
                  pl.BlockSpec((1, 1, f2), lambda i, te, nv: (te[i], 0, 0)),
                  pl.BlockSpec((1, f, d), lambda i, te, nv: (te[i], 0, 0)),
                  pl.BlockSpec((1, 1, d), lambda i, te, nv: (te[i], 0, 0))],
        out_specs=pl.BlockSpec((tm_e, dp), lambda i, te, nv: (i, 0)),
    )
    return pl.pallas_call(
        _mlp_kernel,
        out_shape=jax.ShapeDtypeStruct(hg.shape, jnp.uint32),
        grid_spec=grid_spec,
        compiler_params=_cparams("arbitrary"),
        name="expert_mlp",
    )(te, nv, hg, wgu_bf, bgu.reshape(n_exp, 1, f2), wd_bf, bd.reshape(n_exp, 1, d))


def _combine_kernel(dest_ref, tw_ref, x_ref, m_ref, eo_ref, o_ref, g_ref, sem):
    tt = x_ref.shape[0]

    def row_copy(k, t, d):
        return pltpu.make_async_copy(eo_ref.at[pl.ds(d, 1)], g_ref.at[k, pl.ds(t, 1)], sem)

    def issue(t, carry):
        for k in range(TOP_K):
            row_copy(k, t, dest_ref[0, k, t]).start()
        return carry

    def drain(t, carry):
        for k in range(TOP_K):
            row_copy(k, 0, 0).wait()
        return carry

    lax.fori_loop(0, tt, issue, 0)
    lax.fori_loop(0, tt, drain, 0)
    y = None
    for k in range(TOP_K):
        part = tw_ref[:, k:k + 1] * _unpack_pairs_f32(g_ref[k])
        y = part if y is None else y + part
    o_ref[...] = x_ref[...] + m_ref[0, 5:6, :] * y


def _combine(dest, tw_t, x2, mods, mod_row, eo, *, seq):
    n_tok, d = x2.shape
    dp = eo.shape[1]
    tt = _tile(seq, 256)
    tiles_per_seq = seq // tt
    dest_t = _dest_tiles(dest, tt)
    return pl.pallas_call(
        _combine_kernel,
        out_shape=jax.ShapeDtypeStruct((n_tok, d), F32),
        grid=(n_tok // tt,),
        in_specs=[pl.BlockSpec((1, TOP_K, tt), lambda i: (i, 0, 0), memory_space=pltpu.SMEM),
                  pl.BlockSpec((tt, TOP_K), lambda i: (i, 0)),
                  pl.BlockSpec((tt, d), lambda i: (i, 0)),
                  pl.BlockSpec((1, N_MOD, d), lambda i: (mod_row(i // tiles_per_seq), 0, 0)),
                  pl.BlockSpec(memory_space=pl.ANY)],
        out_specs=pl.BlockSpec((tt, d), lambda i: (i, 0)),
        scratch_shapes=[pltpu.VMEM((TOP_K, tt, dp), jnp.uint32), pltpu.SemaphoreType.DMA(())],
        compiler_params=_cparams("arbitrary"),
        name="moe_combine",
    )(dest_t, tw_t, x2, mods, eo)


def _rope_tables(seq, hd):
    n_freq = hd // 4
    pos = jnp.arange(seq)
    row = (pos // GRID_W).astype(F32)
    col = (pos % GRID_W).astype(F32)
    inv = ROPE_BASE ** (-jnp.arange(n_freq, dtype=F32) / n_freq)
    ang = jnp.stack([row[:, None] * inv, col[:, None] * inv], axis=1)
    ang = jnp.stack([ang, ang], axis=2).reshape(seq, hd)
    first_half = (jnp.arange(hd) % (hd // 2)) < (hd // 4)
    return jnp.cos(ang), jnp.where(first_half, -1.0, 1.0) * jnp.sin(ang)


def kernel(x, c, ctx, c_ctx, w_ada, b_ada, norm_mix_g, norm_ffn_g, w_in, q_norm_g, k_norm_g, lambda_qk,
           subln_g, conv_w, w_attn_out, w_conv_out, w_mix_out, w_router, b_router, w_gate_up, b_gate_up,
           w_down, b_down):
    batch, seq, d = x.shape
    n_ctx = ctx.shape[1]
    depth = w_ada.shape[0]
    hd = q_norm_g.shape[-1]
    n_cols = w_in.shape[-1]
    qkw = (n_cols - 5 * d) // 3
    n_heads = qkw // (2 * hd)
    n_exp = w_router.shape[-1]
    u_off = 3 * qkw
    ga_off = u_off + 3 * d

    cos, sin_s = _rope_tables(seq, hd)
    n_mod_rows = -(-(batch + 1) // 8) * 8
    cv = jnp.zeros((n_mod_rows, d), F32).at[:batch].set(c).at[batch].set(c_ctx)
    mods_all = _adaln(cv, w_ada, b_ada).reshape(depth, n_mod_rows, N_MOD, d)
    lat_row = lambda b: b
    ctx_row = lambda b: batch

    xs = x.reshape(batch * seq, d)
    cs = ctx.reshape(batch * n_ctx, d)
    for l in range(depth):
        last = l == depth - 1
        lam_init = 0.8 - 0.6 * math.exp(-0.3 * l)
        mods = mods_all[l]
        g_mix = norm_mix_g[l][None]
        g_ffn = norm_ffn_g[l][None]
        q_g = q_norm_g[l][None]
        k_g = k_norm_g[l][None]
        sub_g = subln_g[l][None]
        w_in_bf = w_in[l].astype(BF16)

        p_lat = _inproj(xs, mods, lat_row, g_mix, w_in_bf, q_g, k_g, cos, sin_s,
                        qkw=qkw, n_cols=n_cols, seq=seq, rope=True, hd=hd)
        p_ctx = _inproj(cs, mods, ctx_row, g_mix, w_in_bf, q_g, k_g, cos, sin_s,
                        qkw=qkw, n_cols=(3 * qkw if last else n_cols), seq=n_ctx, rope=False, hd=hd)
        attn_args = dict(batch=batch, n_heads=n_heads, hd=hd, qkw=qkw, lam_init=lam_init)
        y_attn = _attention(lambda_qk[l], sub_g, p_lat, [p_ctx, p_lat], q_len=seq,
                            kv_lens=[n_ctx, seq], **attn_args)
        y_conv = _short_conv(p_lat, conv_w[l], batch=batch, seq=seq, u_off=u_off, width=d)
        wa_bf = w_attn_out[l].astype(BF16)
        wc_bf = w_conv_out[l].astype(BF16)
        wm_bf = w_mix_out[l].astype(BF16)
        w_router_t = w_router[l].T
        b_router_c = b_router[l][:, None]
        y_lat = _merge1(y_attn, y_conv, wa_bf, wc_bf, p_lat, ga_off=ga_off)
        xs, hp_lat, ids_lat, tw_lat = _merge2(y_lat, wm_bf, xs, mods, lat_row, g_ffn, w_router_t, b_router_c,
                                              seq=seq)
        if not last:
            y_attn_c = _attention(lambda_qk[l], sub_g, p_ctx, [p_ctx], q_len=n_ctx, kv_lens=[n_ctx],
                                  **attn_args)
            y_conv_c = _short_conv(p_ctx, conv_w[l], batch=batch, seq=n_ctx, u_off=u_off, width=d)
            y_ctx = _merge1(y_attn_c, y_conv_c, wa_bf, wc_bf, p_ctx, ga_off=ga_off)
            cs, hp_ctx, ids_ctx, tw_ctx = _merge2(y_ctx, wm_bf, cs, mods, ctx_row, g_ffn, w_router_t,
                                                  b_router_c, seq=n_ctx)
            ids = jnp.concatenate([ids_ctx, ids_lat], axis=1)
        else:
            ids = ids_lat

        n_tok = ids.shape[1]
        n_pairs = n_tok * TOP_K
        tm_e = 512 if n_pairs >= 32768 else 32
        n_tiles = n_pairs // tm_e + n_exp
        n_tiles_pad = -(-n_tiles // LANES) * LANES
        dest, te, nv = _plan(ids, n_exp=n_exp, tm_e=tm_e, n_tiles_pad=n_tiles_pad)
        n_rows_pad = n_tiles * tm_e
        if not last:
            n_c = batch * n_ctx
            hg = _dispatch(dest[:, :n_c], hp_ctx, None, n_rows_pad=n_rows_pad)
            hg = _dispatch(dest[:, n_c:], hp_lat, hg, n_rows_pad=n_rows_pad)
        else:
            hg = _dispatch(dest, hp_lat, None, n_rows_pad=n_rows_pad)
        eo = _expert_mlp(te[0], nv[0], hg, w_gate_up[l].astype(BF16), b_gate_up[l],
                         w_down[l].astype(BF16), b_down[l], tm_e=tm_e, n_tiles=n_tiles)
        if not last:
            cs = _combine(dest[:, :n_c], tw_ctx.T, cs, mods, ctx_row, eo, seq=n_ctx)
            xs = _combine(dest[:, n_c:], tw_lat.T, xs, mods, lat_row, eo, seq=seq)
        else:
            xs = _combine(dest, tw_lat.T, xs, mods, lat_row, eo, seq=seq)
    return xs.reshape(batch, seq, d)
```

```python
import functools
import math

import jax
import jax.numpy as jnp
from jax import lax
from jax.experimental import pallas as pl
from jax.experimental.pallas import tpu as pltpu

GRID_W = 64
TOP_K = 4
ROPE_BASE = 10000.0
RMS_EPS = 1e-6
SWIGLU_LIMIT = 7.0
SWIGLU_ALPHA = 1.702
N_MOD = 6
LANES = 128
SUBLANES = 8
CAST_BLOCK_ELEMS = 2 ** 21
INPROJ_COL_CHUNK = 512
ATTN_ROW_BLOCK = 128
ATTN_KEY_CHUNK = 256
VMEM_LIMIT_BYTES = 56 * 2 ** 20

F32 = jnp.float32
BF16 = jnp.bfloat16
NT_DIMS = (((1,), (1,)), ((), ()))


def _cparams(*sem):
    return pltpu.CompilerParams(dimension_semantics=sem, vmem_limit_bytes=VMEM_LIMIT_BYTES)


def _tile(n, pref):
    if n <= pref:
        return n
    for t in range(pref, 7, -1):
        if n % t == 0 and t % 8 == 0:
            return t
    return n


def _pack_pairs(x):
    n = x.shape[1] // 2
    hi = lax.bitcast_convert_type(x[:, :n].astype(BF16).astype(F32), jnp.uint32)
    lo = lax.bitcast_convert_type(x[:, n:].astype(BF16).astype(F32), jnp.uint32)
    return hi | (lo >> 16)


def _unpack_pairs_f32(p):
    left = lax.bitcast_convert_type(p & jnp.uint32(0xFFFF0000), F32)
    right = lax.bitcast_convert_type(p << 16, F32)
    return jnp.concatenate([left, right], axis=1)


def _rms(x):
    return x * lax.rsqrt(jnp.mean(x * x, axis=-1, keepdims=True) + RMS_EPS)


def _cast_kernel(w_ref, o_ref):
    o_ref[...] = w_ref[0].astype(BF16)


def _layer_bf16(w, layer):
    cols = w.shape[-1]
    w3 = w.reshape(w.shape[0], -1, cols)
    rows = w3.shape[1]
    tr = _tile(rows, max(8, CAST_BLOCK_ELEMS // cols))
    out = pl.pallas_call(
        _cast_kernel,
        out_shape=jax.ShapeDtypeStruct((rows, cols), BF16),
        grid=(rows // tr,),
        in_specs=[pl.BlockSpec((1, tr, cols), lambda i: (layer, i, 0))],
        out_specs=pl.BlockSpec((tr, cols), lambda i: (i, 0)),
        compiler_params=_cparams("arbitrary"),
        name="cast_bf16",
    )(w3)
    return out.reshape(w.shape[1:])


def _adaln_kernel(cv_ref, w_ref, b_ref, o_ref):
    cv = cv_ref[...]
    a = (cv * jax.nn.sigmoid(cv)).astype(BF16)
    o_ref[0] = jnp.dot(a, w_ref[0].astype(BF16), preferred_element_type=F32) + b_ref[0]


def _adaln(cv, w_ada, b_ada):
    n_layers, d, n = w_ada.shape
    r = cv.shape[0]
    tn = _tile(n, 1024)
    return pl.pallas_call(
        _adaln_kernel,
        out_shape=jax.ShapeDtypeStruct((n_layers, r, n), F32),
        grid=(n_layers, n // tn),
        in_specs=[pl.BlockSpec((r, d), lambda l, j: (0, 0)),
                  pl.BlockSpec((1, d, tn), lambda l, j: (l, 0, j)),
                  pl.BlockSpec((1, 1, tn), lambda l, j: (l, 0, j))],
        out_specs=pl.BlockSpec((1, r, tn), lambda l, j: (l, 0, j)),
        compiler_params=_cparams("arbitrary", "arbitrary"),
        name="adaln",
    )(cv, w_ada, b_ada.reshape(n_layers, 1, n))


def _inproj_kernel(x_ref, m_ref, g_ref, w_ref, qg_ref, kg_ref, cos_ref, sin_ref, o_ref, h_ref,
                   *, nq, rope, hd, q_scale):
    j = pl.program_id(1)

    @pl.when(j == 0)
    def _():
        h = _rms(x_ref[...]) * g_ref[...] * (1.0 + m_ref[0, 1:2, :]) + m_ref[0, 0:1, :]
        h_ref[...] = h.astype(BF16)

    acc = jnp.dot(h_ref[...], w_ref[...], preferred_element_type=F32)
    tn = acc.shape[1]

    def qk_epilogue(gain):
        for g in range(0, tn, hd):
            n = _rms(acc[:, g:g + hd]) * gain
            if rope:
                n = n * cos_ref[...] + pltpu.roll(n, hd // 2, axis=1) * sin_ref[...]
            o_ref[:, g:g + hd] = n.astype(BF16)

    @pl.when(j < nq)
    def _():
        qk_epilogue(qg_ref[...] * q_scale)

    @pl.when((j >= nq) & (j < 2 * nq))
    def _():
        qk_epilogue(kg_ref[...])

    @pl.when(j >= 2 * nq)
    def _():
        o_ref[...] = acc.astype(BF16)


def _inproj(x2, mods, mod_row, norm_g, w_bf, q_g, k_g, cos, sin_s, *, qkw, n_cols, seq, rope, hd):
    rows, d = x2.shape
    tm = _tile(seq, 1024)
    tn = _tile(qkw, 1024)
    tiles_per_seq = seq // tm
    q_scale = float(hd) ** -0.5 * math.log2(math.e)
    kern = functools.partial(_inproj_kernel, nq=qkw // tn, rope=rope, hd=hd, q_scale=q_scale)
    return pl.pallas_call(
        kern,
        out_shape=jax.ShapeDtypeStruct((rows, n_cols), BF16),
        grid=(rows // tm, n_cols // tn),
        in_specs=[pl.BlockSpec((tm, d), lambda i, j: (i, 0)),
                  pl.BlockSpec((1, N_MOD, d), lambda i, j: (mod_row(i // tiles_per_seq), 0, 0)),
                  pl.BlockSpec((1, d), lambda i, j: (0, 0)),
                  pl.BlockSpec((d, tn), lambda i, j: (0, j)),
                  pl.BlockSpec((1, hd), lambda i, j: (0, 0)),
                  pl.BlockSpec((1, hd), lambda i, j: (0, 0)),
                  pl.BlockSpec((tm, hd), lambda i, j: ((i % tiles_per_seq) if rope else 0, 0)),
                  pl.BlockSpec((tm, hd), lambda i, j: ((i % tiles_per_seq) if rope else 0, 0))],
        out_specs=pl.BlockSpec((tm, tn), lambda i, j: (i, j)),
        scratch_shapes=[pltpu.VMEM((tm, d), BF16)],
        compiler_params=_cparams("arbitrary", "arbitrary"),
        name="inproj",
    )(x2, mods, norm_g, w_bf, q_g, k_g, cos, sin_s)


def _attn_kernel(lq_ref, sg_ref, q_ref, *refs, n_src, lam_init, hd):
    k_refs = refs[0:2 * n_src:2]
    v_refs = refs[1:2 * n_src:2]
    o_ref = refs[2 * n_src]
    s_ref = refs[2 * n_src + 1]
    lq = lq_ref[...]
    lam = (jnp.exp(jnp.sum(lq[0:1] * lq[1:2], axis=1, keepdims=True))
           - jnp.exp(jnp.sum(lq[2:3] * lq[3:4], axis=1, keepdims=True)) + lam_init)
    tq = q_ref.shape[0]
    rb = min(tq, ATTN_ROW_BLOCK)
    chunks = []
    col = 0
    for si in range(n_src):
        n = k_refs[si].shape[0]
        for st in range(0, n, ATTN_KEY_CHUNK):
            size = min(ATTN_KEY_CHUNK, n - st)
            chunks.append((si, st, size, col))
            col += size

    def lane_tiles(size):
        step = LANES if size % LANES == 0 else size
        return [(j, step) for j in range(0, size, step)]

    class RowReduce:
        def __init__(self, op, combine):
            self.op, self.combine, self.wide, self.narrow = op, combine, None, None

        def add(self, part):
            if part.shape[1] == LANES:
                self.wide = part if self.wide is None else self.combine(self.wide, part)
            else:
                part = self.op(part, axis=1, keepdims=True)
                self.narrow = part if self.narrow is None else self.combine(self.narrow, part)

        def result(self):
            out = [] if self.narrow is None else [self.narrow]
            if self.wide is not None:
                out.append(self.op(self.wide, axis=1, keepdims=True))
            return functools.reduce(self.combine, out)

    for r0 in range(0, tq, rb):
        maps = []
        for m in range(2):
            qm = q_ref[r0:r0 + rb, m * hd:(m + 1) * hd]
            row_max = RowReduce(jnp.max, jnp.maximum)
            for si, st, size, col in chunks:
                s = lax.dot_general(qm, k_refs[si][st:st + size, m * hd:(m + 1) * hd], NT_DIMS,
                                    preferred_element_type=F32)
                s_ref[m, r0:r0 + rb, col:col + size] = s
                for j, w in lane_tiles(size):
                    row_max.add(s[:, j:j + w])
            mx = row_max.result()
            mx_wide = jnp.broadcast_to(mx, (rb, LANES))
            row_sum = RowReduce(jnp.sum, jnp.add)
            pv = None
            for si, st, size, col in chunks:
                probs = []
                for j, w in lane_tiles(size):
                    e = jnp.exp2(s_ref[m, r0:r0 + rb, col + j:col + j + w] - mx_wide[:, :w])
                    row_sum.add(e)
                    probs.append(e.astype(BF16))
                p = probs[0] if len(probs) == 1 else jnp.concatenate(probs, axis=1)
                d = jnp.dot(p, v_refs[si][st:st + size, :], preferred_element_type=F32)
                pv = d if pv is None else pv + d
            norm = row_sum.result()
            maps.append(pv * (1.0 / norm))
        o = maps[0] - lam * maps[1]
        o = _rms(o) * sg_ref[...] * (1.0 - lam_init)
        o_ref[r0:r0 + rb, :] = o.astype(BF16)


def _attention(lq, sub_g, q_arr, kv_arrs, *, batch, q_len, kv_lens, n_heads, hd, qkw, lam_init):
    vd = 2 * hd
    tq = _tile(q_len, 1024)
    nqb = q_len // tq
    k_off = qkw // vd
    v_off = 2 * qkw // vd
    in_specs = [pl.BlockSpec((4, hd), lambda b, h, i: (0, 0)),
                pl.BlockSpec((1, vd), lambda b, h, i: (0, 0)),
                pl.BlockSpec((tq, vd), lambda b, h, i: (b * nqb + i, h))]
    args = [lq, sub_g, q_arr]
    for arr, n in zip(kv_arrs, kv_lens):
        in_specs.append(pl.BlockSpec((n, vd), lambda b, h, i: (b, k_off + h)))
        in_specs.append(pl.BlockSpec((n, vd), lambda b, h, i: (b, v_off + h)))
        args += [arr, arr]
    kern = functools.partial(_attn_kernel, n_src=len(kv_arrs), lam_init=lam_init, hd=hd)
    return pl.pallas_call(
        kern,
        out_shape=jax.ShapeDtypeStruct((batch * q_len, n_heads * vd), BF16),
        grid=(batch, n_heads, nqb),
        in_specs=in_specs,
        out_specs=pl.BlockSpec((tq, vd), lambda b, h, i: (b * nqb + i, h)),
        scratch_shapes=[pltpu.VMEM((2, tq, sum(kv_lens)), F32)],
        compiler_params=_cparams("arbitrary", "arbitrary", "arbitrary"),
        name="diff_attn",
    )(*args)


def _conv_kernel(u_ref, b_ref, c_ref, w_ref, o_ref):
    cu = c_ref[...].astype(F32) * u_ref[...].astype(F32)
    n = cu.shape[0]
    row = lax.broadcasted_iota(jnp.int32, cu.shape, 0)
    prev = jnp.where(row == 0, 0.0, pltpu.roll(cu, 1, axis=0))
    nxt = jnp.where(row == n - 1, 0.0, pltpu.roll(cu, n - 1, axis=0))
    w = w_ref[...]
    y = b_ref[...].astype(F32) * (w[0:1] * prev + w[1:2] * cu + w[2:3] * nxt)
    o_ref[...] = y.astype(BF16)


def _short_conv(p_arr, conv_w, *, batch, seq, u_off, width):
    tc = _tile(width, 512)
    o = u_off // tc
    nb = width // tc
    return pl.pallas_call(
        _conv_kernel,
        out_shape=jax.ShapeDtypeStruct((batch * seq, width), BF16),
        grid=(batch, nb),
        in_specs=[pl.BlockSpec((seq, tc), lambda b, j: (b, o + j)),
                  pl.BlockSpec((seq, tc), lambda b, j: (b, o + nb + j)),
                  pl.BlockSpec((seq, tc), lambda b, j: (b, o + 2 * nb + j)),
                  pl.BlockSpec((conv_w.shape[0], tc), lambda b, j: (0, j))],
        out_specs=pl.BlockSpec((seq, tc), lambda b, j: (b, j)),
        compiler_params=_cparams("arbitrary", "arbitrary"),
        name="short_conv",
    )(p_arr, p_arr, p_arr, conv_w)


def _merge1_kernel(ya_ref, yc_ref, wa_ref, wc_ref, ga_ref, gb_ref, o_ref):
    a = jnp.dot(ya_ref[...], wa_ref[...], preferred_element_type=F32)
    c = jnp.dot(yc_ref[...], wc_ref[...], preferred_element_type=F32)
    y = jax.nn.sigmoid(ga_ref[...].astype(F32)) * a + jax.nn.sigmoid(gb_ref[...].astype(F32)) * c
    o_ref[...] = y.astype(BF16)


def _merge1(y_attn, y_conv, wa_bf, wc_bf, p_arr, *, ga_off):
    rows, ka = y_attn.shape
    kc = y_conv.shape[1]
    d = wa_bf.shape[1]
    tm = _tile(rows, 1024)
    tn = _tile(d, 512)
    o = ga_off // tn
    nb = d // tn
    return pl.pallas_call(
        _merge1_kernel,
        out_shape=jax.ShapeDtypeStruct((rows, d), BF16),
        grid=(rows // tm, nb),
        in_specs=[pl.BlockSpec((tm, ka), lambda i, j: (i, 0)),
                  pl.BlockSpec((tm, kc), lambda i, j: (i, 0)),
                  pl.BlockSpec((ka, tn), lambda i, j: (0, j)),
                  pl.BlockSpec((kc, tn), lambda i, j: (0, j)),
                  pl.BlockSpec((tm, tn), lambda i, j: (i, o + j)),
                  pl.BlockSpec((tm, tn), lambda i, j: (i, o + nb + j))],
        out_specs=pl.BlockSpec((tm, tn), lambda i, j: (i, j)),
        compiler_params=_cparams("arbitrary", "arbitrary"),
        name="merge_branches",
    )(y_attn, y_conv, wa_bf, wc_bf, p_arr, p_arr)


def _split_bf16(x):
    hi = x.astype(BF16)
    return hi, (x - hi.astype(F32)).astype(BF16)


def _merge2_kernel(y_ref, wm_ref, x_ref, m_ref, gn_ref, wr_ref, br_ref, xo_ref, hp_ref, ids_ref, tw_ref):
    mix = jnp.dot(y_ref[...], wm_ref[...], preferred_element_type=F32)
    xn = x_ref[...] + m_ref[0, 2:3, :] * mix
    xo_ref[...] = xn
    h = _rms(xn) * gn_ref[...] * (1.0 + m_ref[0, 4:5, :]) + m_ref[0, 3:4, :]
    hp_ref[...] = _pack_pairs(h)

    h_hi, h_lo = _split_bf16(h)
    w_hi, w_lo = _split_bf16(wr_ref[...])
    logits = (lax.dot_general(w_hi, h_hi, NT_DIMS, preferred_element_type=F32)
              + lax.dot_general(w_hi, h_lo, NT_DIMS, preferred_element_type=F32)
              + lax.dot_general(w_lo, h_hi, NT_DIMS, preferred_element_type=F32)) + br_ref[...]
    n_exp = logits.shape[0]
    expert = lax.broadcasted_iota(jnp.int32, logits.shape, 0)
    vals = []
    for k in range(TOP_K):
        mx = jnp.max(logits, axis=0, keepdims=True)
        idx = jnp.min(jnp.where(logits == mx, expert, n_exp), axis=0, keepdims=True)
        ids_ref[k:k + 1, :] = idx
        vals.append(mx)
        logits = jnp.where(expert == idx, -jnp.inf, logits)
    es = [jnp.exp(v - vals[0]) for v in vals]
    inv = 1.0 / functools.reduce(jnp.add, es)
    for k in range(TOP_K):
        tw_ref[k:k + 1, :] = es[k] * inv


def _merge2(y, wm_bf, x2, mods, mod_row, norm_g, w_router_t, b_router, *, seq):
    rows, d = x2.shape
    n_exp = w_router_t.shape[0]
    tm = _tile(seq, 512)
    tiles_per_seq = seq // tm
    return pl.pallas_call(
        _merge2_kernel,
        out_shape=(jax.ShapeDtypeStruct((rows, d), F32),
                   jax.ShapeDtypeStruct((rows, d // 2), jnp.uint32),
                   jax.ShapeDtypeStruct((TOP_K, rows), jnp.int32),
                   jax.ShapeDtypeStruct((TOP_K, rows), F32)),
        grid=(rows // tm,),
        in_specs=[pl.BlockSpec((tm, d), lambda i: (i, 0)),
                  pl.BlockSpec((d, d), lambda i: (0, 0)),
                  pl.BlockSpec((tm, d), lambda i: (i, 0)),
                  pl.BlockSpec((1, N_MOD, d), lambda i: (mod_row(i // tiles_per_seq), 0, 0)),
                  pl.BlockSpec((1, d), lambda i: (0, 0)),
                  pl.BlockSpec((n_exp, d), lambda i: (0, 0)),
                  pl.BlockSpec((n_exp, 1), lambda i: (0, 0))],
        out_specs=(pl.BlockSpec((tm, d), lambda i: (i, 0)),
                   pl.BlockSpec((tm, d // 2), lambda i: (i, 0)),
                   pl.BlockSpec((TOP_K, tm), lambda i: (0, i)),
                   pl.BlockSpec((TOP_K, tm), lambda i: (0, i))),
        compiler_params=_cparams("arbitrary"),
        name="mixout_router",
    )(y, wm_bf, x2, mods, norm_g, w_router_t, b_router)


def _plan_kernel(ids_ref, dest_ref, te_ref, nv_ref, cnt_ref, run_ref, gs_ref, *, n_exp, tm_e):
    p = pl.program_id(0)
    t = pl.program_id(1)
    ids = ids_ref[...]
    tt = ids.shape[1]
    expert = lax.broadcasted_iota(jnp.int32, (n_exp, tt), 0)
    hit = [ids[k:k + 1, :] == expert for k in range(TOP_K)]
    member = functools.reduce(jnp.add, [h.astype(F32) for h in hit])
    tile_count = jnp.sum(member, axis=1, keepdims=True)

    @pl.when((p == 0) & (t == 0))
    def _():
        cnt_ref[...] = jnp.zeros_like(cnt_ref)

    @pl.when(p == 0)
    def _():
        cnt_ref[...] += tile_count

    @pl.when((p == 1) & (t == 0))
    def _():
        cnt = cnt_ref[...]
        ptiles = jnp.floor((cnt + (tm_e - 1)) * (1.0 / tm_e))
        r = lax.broadcasted_iota(jnp.int32, (n_exp, n_exp), 0)
        cidx = lax.broadcasted_iota(jnp.int32, (n_exp, n_exp), 1)
        lower = (cidx < r).astype(BF16)
        p_hi = jnp.floor(ptiles * (1.0 / 16.0))
        p_lo = ptiles - 16.0 * p_hi
        gstart = (16.0 * jnp.dot(lower, p_hi.astype(BF16), preferred_element_type=F32)
                  + jnp.dot(lower, p_lo.astype(BF16), preferred_element_type=F32))
        gs_ref[...] = gstart
        run_ref[...] = jnp.zeros_like(run_ref)
        ntp = te_ref.shape[1]
        tile = lax.broadcasted_iota(jnp.int32, (n_exp, ntp), 1).astype(F32)
        e2 = lax.broadcasted_iota(jnp.int32, (n_exp, ntp), 0).astype(F32)
        gend = gstart[:, 0:1] + ptiles[:, 0:1]
        te = jnp.sum((gend <= tile).astype(F32), axis=0, keepdims=True)
        used = te < n_exp
        te = jnp.minimum(te, n_exp - 1.0)
        onehot = (e2 == te).astype(F32)
        start_e = jnp.sum(onehot * gstart[:, 0:1], axis=0, keepdims=True)
        cnt_e = jnp.sum(onehot * cnt[:, 0:1], axis=0, keepdims=True)
        nv = jnp.clip(cnt_e - (tile[0:1] - start_e) * tm_e, 0.0, float(tm_e))
        te_ref[...] = te.astype(jnp.int32)
        nv_ref[...] = jnp.where(used, nv, 0.0).astype(jnp.int32)

    @pl.when(p == 1)
    def _():
        rr = lax.broadcasted_iota(jnp.int32, (tt, tt), 0)
        cc = lax.broadcasted_iota(jnp.int32, (tt, tt), 1)
        upper = (rr < cc).astype(BF16)
        before = jnp.dot(member.astype(BF16), upper, preferred_element_type=F32)
        pos = gs_ref[:, 0:1] * tm_e + run_ref[:, 0:1] + before
        for k in range(TOP_K):
            d = jnp.sum(jnp.where(hit[k], pos, 0.0), axis=0, keepdims=True)
            dest_ref[k:k + 1, :] = d.astype(jnp.int32)
        run_ref[...] += tile_count


def _plan(ids, *, n_exp, tm_e, n_tiles_pad):
    n_tok = ids.shape[1]
    tt = _tile(n_tok, 512)
    kern = functools.partial(_plan_kernel, n_exp=n_exp, tm_e=tm_e)
    return pl.pallas_call(
        kern,
        out_shape=(jax.ShapeDtypeStruct((TOP_K, n_tok), jnp.int32),
                   jax.ShapeDtypeStruct((1, n_tiles_pad), jnp.int32),
                   jax.ShapeDtypeStruct((1, n_tiles_pad), jnp.int32)),
        grid=(2, n_tok // tt),
        in_specs=[pl.BlockSpec((TOP_K, tt), lambda p, t: (0, t))],
        out_specs=(pl.BlockSpec((TOP_K, tt), lambda p, t: (0, p * t)),
                   pl.BlockSpec((1, n_tiles_pad), lambda p, t: (0, 0)),
                   pl.BlockSpec((1, n_tiles_pad), lambda p, t: (0, 0))),
        scratch_shapes=[pltpu.VMEM((n_exp, LANES), F32)] * 3,
        compiler_params=_cparams("arbitrary", "arbitrary"),
        name="route_plan",
    )(ids)


def _dest_tiles(dest, tt):
    k, n = dest.shape
    return dest.T.reshape(n // tt, 1, tt * k)


def _dispatch_body(dest_ref, h_ref, hg_ref, sem):
    groups = h_ref.shape[0]

    def issue(g, carry):
        for s in range(SUBLANES):
            for k in range(TOP_K):
                d = dest_ref[0, 0, (g * SUBLANES + s) * TOP_K + k]
                pltpu.make_async_copy(h_ref.at[g, pl.ds(s, 1)], hg_ref.at[pl.ds(d, 1)],
                                      sem.at[k % 2]).start(priority=k % 2)
        return carry

    lax.fori_loop(0, groups, issue, 0)
    for p in range(2):
        for _ in range(TOP_K // 2):
            pltpu.make_async_copy(h_ref, h_ref, sem.at[p]).wait()


def _dispatch_first_kernel(dest_ref, h_ref, hg_ref, sem):
    _dispatch_body(dest_ref, h_ref, hg_ref, sem)


def _dispatch_next_kernel(dest_ref, h_ref, hg_in_ref, hg_ref, sem):
    del hg_in_ref
    _dispatch_body(dest_ref, h_ref, hg_ref, sem)


def _dispatch(dest, hp, hg, *, n_rows_pad):
    n_tok, dp = hp.shape
    tt = _tile(n_tok, 512)
    dest_t = _dest_tiles(dest, tt)
    in_specs = [pl.BlockSpec((1, 1, tt * TOP_K), lambda i: (i, 0, 0), memory_space=pltpu.SMEM),
                pl.BlockSpec((tt // SUBLANES, SUBLANES, dp), lambda i: (i, 0, 0))]
    args = [dest_t, hp.reshape(n_tok // SUBLANES, SUBLANES, dp)]
    aliases = {}
    kern = _dispatch_first_kernel
    if hg is not None:
        in_specs.append(pl.BlockSpec(memory_space=pl.ANY))
        args.append(hg)
        aliases = {2: 0}
        kern = _dispatch_next_kernel
    return pl.pallas_call(
        kern,
        out_shape=jax.ShapeDtypeStruct((n_rows_pad, dp), jnp.uint32),
        grid=(n_tok // tt,),
        in_specs=in_specs,
        out_specs=pl.BlockSpec(memory_space=pl.ANY),
        scratch_shapes=[pltpu.SemaphoreType.DMA((2,))],
        input_output_aliases=aliases,
        compiler_params=pltpu.CompilerParams(dimension_semantics=("arbitrary",),
                                             vmem_limit_bytes=VMEM_LIMIT_BYTES, has_side_effects=True),
        name="moe_dispatch",
    )(*args)


def _mlp_kernel(te_ref, nv_ref, hg_ref, wgu_ref, bgu_ref, wd_ref, bd_ref, o_ref):
    del te_ref
    nv = nv_ref[pl.program_id(0)]

    @pl.when(nv == 0)
    def _():
        o_ref[...] = jnp.zeros_like(o_ref)

    @pl.when(nv > 0)
    def _():
        f = wd_ref.shape[1]
        packed = hg_ref[...]
        row = lax.broadcasted_iota(jnp.int32, packed.shape, 0)
        packed = jnp.where(row < nv, packed, jnp.uint32(0))
        h = _unpack_pairs_f32(packed).astype(BF16)
        gu = jnp.dot(h, wgu_ref[0], preferred_element_type=F32) + bgu_ref[0]
        gate = jnp.minimum(gu[:, :f], SWIGLU_LIMIT)
        lin = jnp.clip(gu[:, f:], -SWIGLU_LIMIT, SWIGLU_LIMIT)
        act = gate * jax.nn.sigmoid(SWIGLU_ALPHA * gate) * (lin + 1.0)
        eo = jnp.dot(act.astype(BF16), wd_ref[0], preferred_element_type=F32) + bd_ref[0]
        o_ref[...] = _pack_pairs(eo)


def _expert_mlp(te, nv, hg, wgu_bf, bgu, wd_bf, bd, *, tm_e, n_tiles):
    n_exp, d, f2 = wgu_bf.shape
    f = wd_bf.shape[1]
    dp = hg.shape[1]
    grid_spec = pltpu.PrefetchScalarGridSpec(
        num_scalar_prefetch=2,
        grid=(n_tiles,),
        in_specs=[pl.BlockSpec((tm_e, dp), lambda i, te, nv: (i, 0)),
                  pl.BlockSpec((1, d, f2), lambda i, te, nv: (te[i], 0, 0)),
                  pl.BlockSpec((1, 1, f2), lambda i, te, nv: (te[i], 0, 0)),
                  pl.BlockSpec((1, f, d), lambda i, te, nv: (te[i], 0, 0)),
                  pl.BlockSpec((1, 1, d), lambda i, te, nv: (te[i], 0, 0))],
        out_specs=pl.BlockSpec((tm_e, dp), lambda i, te, nv: (i, 0)),
    )
    return pl.pallas_call(
        _mlp_kernel,
        out_shape=jax.ShapeDtypeStruct(hg.shape, jnp.uint32),
        grid_spec=grid_spec,
        compiler_params=_cparams("arbitrary"),
        name="expert_mlp",
    )(te, nv, hg, wgu_bf, bgu.reshape(n_exp, 1, f2), wd_bf, bd.reshape(n_exp, 1, d))


def _combine_kernel(dest_ref, dest_next_ref, tw_ref, x_ref, m_ref, eo_ref, o_ref, g_ref, sem):
    tt = x_ref.shape[0]
    groups = tt // SUBLANES
    dp = g_ref.shape[-1]
    i = pl.program_id(0)
    slot = i % 2

    def issue_all(idx_ref, sl):
        def issue(g, carry):
            for s in range(SUBLANES):
                for k in range(TOP_K):
                    d = idx_ref[0, 0, (g * SUBLANES + s) * TOP_K + k]
                    pltpu.make_async_copy(eo_ref.at[pl.ds(d, 1)], g_ref.at[sl, k, g, pl.ds(s, 1)],
                                          sem.at[sl]).start(priority=k % 2)
            return carry
        lax.fori_loop(0, groups, issue, 0)

    @pl.when(i == 0)
    def _():
        issue_all(dest_ref, 0)

    @pl.when(i + 1 < pl.num_programs(0))
    def _():
        issue_all(dest_next_ref, 1 - slot)

    pltpu.make_async_copy(g_ref.at[slot], g_ref.at[slot], sem.at[slot]).wait()
    y = None
    for k in range(TOP_K):
        part = tw_ref[:, k:k + 1] * _unpack_pairs_f32(g_ref[slot, k].reshape(tt, dp))
        y = part if y is None else y + part
    o_ref[...] = x_ref[...] + m_ref[0, 5:6, :] * y


def _combine(dest, tw_t, x2, mods, mod_row, eo, *, seq):
    n_tok, d = x2.shape
    dp = eo.shape[1]
    tt = _tile(seq, 256)
    tiles_per_seq = seq // tt
    dest_t = _dest_tiles(dest, tt)
    n_steps = n_tok // tt
    return pl.pallas_call(
        _combine_kernel,
        out_shape=jax.ShapeDtypeStruct((n_tok, d), F32),
        grid=(n_steps,),
        in_specs=[pl.BlockSpec((1, 1, tt * TOP_K), lambda i: (i, 0, 0), memory_space=pltpu.SMEM),
                  pl.BlockSpec((1, 1, tt * TOP_K), lambda i: (jnp.minimum(i + 1, n_steps - 1), 0, 0),
                               memory_space=pltpu.SMEM),
                  pl.BlockSpec((tt, TOP_K), lambda i: (i, 0)),
                  pl.BlockSpec((tt, d), lambda i: (i, 0)),
                  pl.BlockSpec((1, N_MOD, d), lambda i: (mod_row(i // tiles_per_seq), 0, 0)),
                  pl.BlockSpec(memory_space=pl.ANY)],
        out_specs=pl.BlockSpec((tt, d), lambda i: (i, 0)),
        scratch_shapes=[pltpu.VMEM((2, TOP_K, tt // SUBLANES, SUBLANES, dp), jnp.uint32),
                        pltpu.SemaphoreType.DMA((2,))],
        compiler_params=_cparams("arbitrary"),
        name="moe_combine",
    )(dest_t, dest_t, tw_t, x2, mods, eo)


def _rope_tables(seq, hd):
    n_freq = hd // 4
    pos = jnp.arange(seq)
    row = (pos // GRID_W).astype(F32)
    col = (pos % GRID_W).astype(F32)
    inv = ROPE_BASE ** (-jnp.arange(n_freq, dtype=F32) / n_freq)
    ang = jnp.concatenate([row[:, None] * inv, col[:, None] * inv], axis=1)
    ang = jnp.concatenate([ang, ang], axis=1)
    sign = jnp.where(jnp.arange(hd) < hd // 2, -1.0, 1.0)
    return jnp.cos(ang), sign * jnp.sin(ang)


def _half_major(a, hd):
    lead = a.shape[:-1]
    a = a.reshape(*lead, a.shape[-1] // hd, 2, 2, hd // 4)
    return jnp.swapaxes(a, -3, -2).reshape(*lead, -1)


def kernel(x, c, ctx, c_ctx, w_ada, b_ada, norm_mix_g, norm_ffn_g, w_in, q_norm_g, k_norm_g, lambda_qk,
           subln_g, conv_w, w_attn_out, w_conv_out, w_mix_out, w_router, b_router, w_gate_up, b_gate_up,
           w_down, b_down):
    batch, seq, d = x.shape
    n_ctx = ctx.shape[1]
    depth = w_ada.shape[0]
    hd = q_norm_g.shape[-1]
    n_cols = w_in.shape[-1]
    qkw = (n_cols - 5 * d) // 3
    n_heads = qkw // (2 * hd)
    n_exp = w_router.shape[-1]
    u_off = 3 * qkw
    ga_off = u_off + 3 * d

    cos, sin_s = _rope_tables(seq, hd)
    n_mod_rows = -(-(batch + 1) // 8) * 8
    cv = jnp.zeros((n_mod_rows, d), F32).at[:batch].set(c).at[batch].set(c_ctx)
    mods_all = _adaln(cv, w_ada, b_ada).reshape(depth, n_mod_rows, N_MOD, d)
    lat_row = lambda b: b
    ctx_row = lambda b: batch

    xs = x.reshape(batch * seq, d)
    cs = ctx.reshape(batch * n_ctx, d)
    for l in range(depth):
        last = l == depth - 1
        lam_init = 0.8 - 0.6 * math.exp(-0.3 * l)
        mods = mods_all[l]
        g_mix = norm_mix_g[l][None]
        g_ffn = norm_ffn_g[l][None]
        q_g = _half_major(q_norm_g[l][None], hd)
        k_g = _half_major(k_norm_g[l][None], hd)
        sub_g = subln_g[l][None]
        w_in_bf = _layer_bf16(w_in, l)
        w_in_bf = w_in_bf.at[:, :2 * qkw].set(_half_major(w_in_bf[:, :2 * qkw], hd))

        p_lat = _inproj(xs, mods, lat_row, g_mix, w_in_bf, q_g, k_g, cos, sin_s,
                        qkw=qkw, n_cols=n_cols, seq=seq, rope=True, hd=hd)
        p_ctx = _inproj(cs, mods, ctx_row, g_mix, w_in_bf, q_g, k_g, cos, sin_s,
                        qkw=qkw, n_cols=(3 * qkw if last else n_cols), seq=n_ctx, rope=False, hd=hd)
        attn_args = dict(batch=batch, n_heads=n_heads, hd=hd, qkw=qkw, lam_init=lam_init)
        y_attn = _attention(lambda_qk[l], sub_g, p_lat, [p_ctx, p_lat], q_len=seq,
                            kv_lens=[n_ctx, seq], **attn_args)
        y_conv = _short_conv(p_lat, conv_w[l], batch=batch, seq=seq, u_off=u_off, width=d)
        wa_bf = _layer_bf16(w_attn_out, l)
        wc_bf = _layer_bf16(w_conv_out, l)
        wm_bf = _layer_bf16(w_mix_out, l)
        w_router_t = w_router[l].T
        b_router_c = b_router[l][:, None]
        y_lat = _merge1(y_attn, y_conv, wa_bf, wc_bf, p_lat, ga_off=ga_off)
        xs, hp_lat, ids_lat, tw_lat = _merge2(y_lat, wm_bf, xs, mods, lat_row, g_ffn, w_router_t, b_router_c,
                                              seq=seq)
        if not last:
            y_attn_c = _attention(lambda_qk[l], sub_g, p_ctx, [p_ctx], q_len=n_ctx, kv_lens=[n_ctx],
                                  **attn_args)
            y_conv_c = _short_conv(p_ctx, conv_w[l], batch=batch, seq=n_ctx, u_off=u_off, width=d)
            y_ctx = _merge1(y_attn_c, y_conv_c, wa_bf, wc_bf, p_ctx, ga_off=ga_off)
            cs, hp_ctx, ids_ctx, tw_ctx = _merge2(y_ctx, wm_bf, cs, mods, ctx_row, g_ffn, w_router_t,
                                                  b_router_c, seq=n_ctx)
            ids = jnp.concatenate([ids_ctx, ids_lat], axis=1)
        else:
            ids = ids_lat

        n_tok = ids.shape[1]
        n_pairs = n_tok * TOP_K
        tm_e = 512 if n_pairs >= 32768 else 32
        n_tiles = n_pairs // tm_e + n_exp
        n_tiles_pad = -(-n_tiles // LANES) * LANES
        dest, te, nv = _plan(ids, n_exp=n_exp, tm_e=tm_e, n_tiles_pad=n_tiles_pad)
        n_rows_pad = n_tiles * tm_e
        if not last:
            n_c = batch * n_ctx
            hg = _dispatch(dest[:, :n_c], hp_ctx, None, n_rows_pad=n_rows_pad)
            hg = _dispatch(dest[:, n_c:], hp_lat, hg, n_rows_pad=n_rows_pad)
        else:
            hg = _dispatch(dest, hp_lat, None, n_rows_pad=n_rows_pad)
        eo = _expert_mlp(te[0], nv[0], hg, _layer_bf16(w_gate_up, l), b_gate_up[l],
                         _layer_bf16(w_down, l), b_down[l], tm_e=tm_e, n_tiles=n_tiles)
        if not last:
            cs = _combine(dest[:, :n_c], tw_ctx.T, cs, mods, ctx_row, eo, seq=n_ctx)
            xs = _combine(dest[:, n_c:], tw_lat.T, xs, mods, lat_row, eo, seq=seq)
        else:
            xs = _combine(dest, tw_lat.T, xs, mods, lat_row, eo, seq=seq)
    return xs.reshape(batch, seq, d)
```

```python
import functools
import math

import jax
import jax.numpy as jnp
from jax import lax
from jax.experimental import pallas as pl
from jax.experimental.pallas import tpu as pltpu

GRID_W = 64
TOP_K = 4
ROPE_BASE = 10000.0
RMS_EPS = 1e-6
SWIGLU_LIMIT = 7.0
SWIGLU_ALPHA = 1.702
N_MOD = 6
LANES = 128
SUBLANES = 8
CAST_BLOCK_ELEMS = 2 ** 21
INPROJ_ROW_BLOCK = 256
MERGE_ROW_BLOCK = 256
MIXOUT_ROW_BLOCK = 128
ATTN_ROW_BLOCK = 128
ATTN_KEY_CHUNK = 256
ATTN_SCORE_SLOTS = 8
VMEM_LIMIT_BYTES = 56 * 2 ** 20

F32 = jnp.float32
BF16 = jnp.bfloat16
NT_DIMS = (((1,), (1,)), ((), ()))


def _cparams(*sem):
    return pltpu.CompilerParams(dimension_semantics=sem, vmem_limit_bytes=VMEM_LIMIT_BYTES)


def _tile(n, pref):
    if n <= pref:
        return n
    for t in range(pref, 7, -1):
        if n % t == 0 and t % 8 == 0:
            return t
    return n


def _pack_pairs(x):
    n = x.shape[1] // 2
    hi = lax.bitcast_convert_type(x[:, :n].astype(BF16).astype(F32), jnp.uint32)
    lo = lax.bitcast_convert_type(x[:, n:].astype(BF16).astype(F32), jnp.uint32)
    return hi | (lo >> 16)


def _unpack_pairs_f32(p):
    left = lax.bitcast_convert_type(p & jnp.uint32(0xFFFF0000), F32)
    right = lax.bitcast_convert_type(p << 16, F32)
    return jnp.concatenate([left, right], axis=1)


def _rms(x):
    return x * lax.rsqrt(jnp.mean(x * x, axis=-1, keepdims=True) + RMS_EPS)


def _cast_kernel(w_ref, o_ref):
    o_ref[...] = w_ref[0].astype(BF16)


def _layer_bf16(w, layer):
    cols = w.shape[-1]
    w3 = w.reshape(w.shape[0], -1, cols)
    rows = w3.shape[1]
    tr = _tile(rows, max(8, CAST_BLOCK_ELEMS // cols))
    out = pl.pallas_call(
        _cast_kernel,
        out_shape=jax.ShapeDtypeStruct((rows, cols), BF16),
        grid=(rows // tr,),
        in_specs=[pl.BlockSpec((1, tr, cols), lambda i: (layer, i, 0))],
        out_specs=pl.BlockSpec((tr, cols), lambda i: (i, 0)),
        compiler_params=_cparams("arbitrary"),
        name="cast_bf16",
    )(w3)
    return out.reshape(w.shape[1:])


def _adaln_kernel(cv_ref, w_ref, b_ref, o_ref):
    cv = cv_ref[...]
    a = (cv * jax.nn.sigmoid(cv)).astype(BF16)
    o_ref[0] = jnp.dot(a, w_ref[0].astype(BF16), preferred_element_type=F32) + b_ref[0]


def _adaln(cv, w_ada, b_ada):
    n_layers, d, n = w_ada.shape
    r = cv.shape[0]
    tn = _tile(n, 1024)
    return pl.pallas_call(
        _adaln_kernel,
        out_shape=jax.ShapeDtypeStruct((n_layers, r, n), F32),
        grid=(n_layers, n // tn),
        in_specs=[pl.BlockSpec((r, d), lambda l, j: (0, 0)),
                  pl.BlockSpec((1, d, tn), lambda l, j: (l, 0, j)),
                  pl.BlockSpec((1, 1, tn), lambda l, j: (l, 0, j))],
        out_specs=pl.BlockSpec((1, r, tn), lambda l, j: (l, 0, j)),
        compiler_params=_cparams("arbitrary", "arbitrary"),
        name="adaln",
    )(cv, w_ada, b_ada.reshape(n_layers, 1, n))


def _inproj_kernel(x_ref, m_ref, g_ref, w_ref, qg_ref, kg_ref, cos_ref, sin_ref, o_ref, h_ref,
                   *, nq, rope, hd, q_scale):
    j = pl.program_id(1)

    tm, tn = o_ref.shape
    rb = min(tm, INPROJ_ROW_BLOCK)

    def qk_step(gain, first):
        if first:
            for r0 in range(0, tm, rb):
                h = _rms(x_ref[r0:r0 + rb, :]) * g_ref[...] * (1.0 + m_ref[0, 1:2, :]) + m_ref[0, 0:1, :]
                h_ref[r0:r0 + rb, :] = h.astype(BF16)
        accs = [jnp.dot(h_ref[r0:r0 + rb, :], w_ref[...], preferred_element_type=F32) for r0 in range(0, tm, rb)]
        for r0, acc in zip(range(0, tm, rb), accs):
            for g in range(0, tn, hd):
                n = _rms(acc[:, g:g + hd]) * gain
                if rope:
                    n = (n * cos_ref[r0:r0 + rb, :]
                         + pltpu.roll(n, hd // 2, axis=1) * sin_ref[r0:r0 + rb, :])
                o_ref[r0:r0 + rb, g:g + hd] = n.astype(BF16)

    @pl.when(j == 0)
    def _():
        qk_step(qg_ref[...] * q_scale, True)

    @pl.when((j > 0) & (j < nq))
    def _():
        qk_step(qg_ref[...] * q_scale, False)

    @pl.when((j >= nq) & (j < 2 * nq))
    def _():
        qk_step(kg_ref[...], False)

    @pl.when(j >= 2 * nq)
    def _():
        o_ref[...] = jnp.dot(h_ref[...], w_ref[...], preferred_element_type=F32).astype(BF16)


def _inproj(x2, mods, mod_row, norm_g, w_bf, q_g, k_g, cos, sin_s, *, qkw, n_cols, seq, rope, hd):
    rows, d = x2.shape
    tm = _tile(seq, 1024)
    tn = _tile(qkw, 1024)
    tiles_per_seq = seq // tm
    q_scale = float(hd) ** -0.5 * math.log2(math.e)
    kern = functools.partial(_inproj_kernel, nq=qkw // tn, rope=rope, hd=hd, q_scale=q_scale)
    return pl.pallas_call(
        kern,
        out_shape=jax.ShapeDtypeStruct((rows, n_cols), BF16),
        grid=(rows // tm, n_cols // tn),
        in_specs=[pl.BlockSpec((tm, d), lambda i, j: (i, 0)),
                  pl.BlockSpec((1, N_MOD, d), lambda i, j: (mod_row(i // tiles_per_seq), 0, 0)),
                  pl.BlockSpec((1, d), lambda i, j: (0, 0)),
                  pl.BlockSpec((d, tn), lambda i, j: (0, j)),
                  pl.BlockSpec((1, hd), lambda i, j: (0, 0)),
                  pl.BlockSpec((1, hd), lambda i, j: (0, 0)),
                  pl.BlockSpec((tm, hd), lambda i, j: ((i % tiles_per_seq) if rope else 0, 0)),
                  pl.BlockSpec((tm, hd), lambda i, j: ((i % tiles_per_seq) if rope else 0, 0))],
        out_specs=pl.BlockSpec((tm, tn), lambda i, j: (i, j)),
        scratch_shapes=[pltpu.VMEM((tm, d), BF16)],
        compiler_params=_cparams("arbitrary", "arbitrary"),
        name="inproj",
    )(x2, mods, norm_g, w_bf, q_g, k_g, cos, sin_s)


def _attn_kernel(lq_ref, sg_ref, q_ref, *refs, n_src, lam_init, hd):
    k_refs = refs[0:2 * n_src:2]
    v_refs = refs[1:2 * n_src:2]
    o_ref = refs[2 * n_src]
    s_ref = refs[2 * n_src + 1]
    lq = lq_ref[...]
    lam = (jnp.exp(jnp.sum(lq[0:1] * lq[1:2], axis=1, keepdims=True))
           - jnp.exp(jnp.sum(lq[2:3] * lq[3:4], axis=1, keepdims=True)) + lam_init)
    tq = q_ref.shape[0]
    rb = min(tq, ATTN_ROW_BLOCK)
    chunks = []
    col = 0
    for si in range(n_src):
        n = k_refs[si].shape[0]
        for st in range(0, n, ATTN_KEY_CHUNK):
            size = min(ATTN_KEY_CHUNK, n - st)
            chunks.append((si, st, size, col))
            col += size

    def lane_tiles(size):
        step = LANES if size % LANES == 0 else size
        return [(j, step) for j in range(0, size, step)]

    class RowReduce:
        def __init__(self, op, combine):
            self.op, self.combine, self.wide, self.narrow = op, combine, None, None

        def add(self, part):
            if part.shape[1] == LANES:
                self.wide = part if self.wide is None else self.combine(self.wide, part)
            else:
                part = self.op(part, axis=1, keepdims=True)
                self.narrow = part if self.narrow is None else self.combine(self.narrow, part)

        def result(self):
            out = [] if self.narrow is None else [self.narrow]
            if self.wide is not None:
                out.append(self.op(self.wide, axis=1, keepdims=True))
            return functools.reduce(self.combine, out)

    for r0 in range(0, tq, rb):
        maps = []
        for m in range(2):
            qm = q_ref[r0:r0 + rb, m * hd:(m + 1) * hd]
            slot = (2 * (r0 // rb) + m) % s_ref.shape[0]
            row_max = RowReduce(jnp.max, jnp.maximum)
            for si, st, size, col in chunks:
                s = lax.dot_general(qm, k_refs[si][st:st + size, m * hd:(m + 1) * hd], NT_DIMS,
                                    preferred_element_type=F32)
                s_ref[slot, :, col:col + size] = s
                for j, w in lane_tiles(size):
                    row_max.add(s[:, j:j + w])
            mx = row_max.result()
            mx_wide = jnp.broadcast_to(mx, (rb, LANES))
            row_sum = RowReduce(jnp.sum, jnp.add)
            pv = None
            for si, st, size, col in chunks:
                probs = []
                for j, w in lane_tiles(size):
                    e = jnp.exp2(s_ref[slot, :, col + j:col + j + w] - mx_wide[:, :w])
                    row_sum.add(e)
                    probs.append(e.astype(BF16))
                p = probs[0] if len(probs) == 1 else jnp.concatenate(probs, axis=1)
                d = jnp.dot(p, v_refs[si][st:st + size, :], preferred_element_type=F32)
                pv = d if pv is None else pv + d
            norm = row_sum.result()
            maps.append(pv * (1.0 / norm))
        o = maps[0] - lam * maps[1]
        o = _rms(o) * sg_ref[...] * (1.0 - lam_init)
        o_ref[r0:r0 + rb, :] = o.astype(BF16)


def _attention(lq, sub_g, q_arr, kv_arrs, *, batch, q_len, kv_lens, n_heads, hd, qkw, lam_init):
    vd = 2 * hd
    tq = _tile(q_len, 2048)
    rb = min(tq, ATTN_ROW_BLOCK)
    nqb = q_len // tq
    k_off = qkw // vd
    v_off = 2 * qkw // vd
    in_specs = [pl.BlockSpec((4, hd), lambda b, h, i: (0, 0)),
                pl.BlockSpec((1, vd), lambda b, h, i: (0, 0)),
                pl.BlockSpec((tq, vd), lambda b, h, i: (b * nqb + i, h))]
    args = [lq, sub_g, q_arr]
    for arr, n in zip(kv_arrs, kv_lens):
        in_specs.append(pl.BlockSpec((n, vd), lambda b, h, i: (b, k_off + h)))
        in_specs.append(pl.BlockSpec((n, vd), lambda b, h, i: (b, v_off + h)))
        args += [arr, arr]
    kern = functools.partial(_attn_kernel, n_src=len(kv_arrs), lam_init=lam_init, hd=hd)
    return pl.pallas_call(
        kern,
        out_shape=jax.ShapeDtypeStruct((batch * q_len, n_heads * vd), BF16),
        grid=(batch, n_heads, nqb),
        in_specs=in_specs,
        out_specs=pl.BlockSpec((tq, vd), lambda b, h, i: (b * nqb + i, h)),
        scratch_shapes=[pltpu.VMEM((min(ATTN_SCORE_SLOTS, 2 * tq // rb), rb, sum(kv_lens)), F32)],
        compiler_params=_cparams("arbitrary", "arbitrary", "arbitrary"),
        name="diff_attn",
    )(*args)


def _conv_kernel(u_ref, b_ref, c_ref, w_ref, o_ref):
    cu = c_ref[...].astype(F32) * u_ref[...].astype(F32)
    n = cu.shape[0]
    row = lax.broadcasted_iota(jnp.int32, cu.shape, 0)
    prev = jnp.where(row == 0, 0.0, pltpu.roll(cu, 1, axis=0))
    nxt = jnp.where(row == n - 1, 0.0, pltpu.roll(cu, n - 1, axis=0))
    w = w_ref[...]
    y = b_ref[...].astype(F32) * (w[0:1] * prev + w[1:2] * cu + w[2:3] * nxt)
    o_ref[...] = y.astype(BF16)


def _short_conv(p_arr, conv_w, *, batch, seq, u_off, width):
    tc = _tile(width, 512)
    o = u_off // tc
    nb = width // tc
    return pl.pallas_call(
        _conv_kernel,
        out_shape=jax.ShapeDtypeStruct((batch * seq, width), BF16),
        grid=(batch, nb),
        in_specs=[pl.BlockSpec((seq, tc), lambda b, j: (b, o + j)),
                  pl.BlockSpec((seq, tc), lambda b, j: (b, o + nb + j)),
                  pl.BlockSpec((seq, tc), lambda b, j: (b, o + 2 * nb + j)),
                  pl.BlockSpec((conv_w.shape[0], tc), lambda b, j: (0, j))],
        out_specs=pl.BlockSpec((seq, tc), lambda b, j: (b, j)),
        compiler_params=_cparams("arbitrary", "arbitrary"),
        name="short_conv",
    )(p_arr, p_arr, p_arr, conv_w)


def _merge1_kernel(ya_ref, yc_ref, wa_ref, wc_ref, ga_ref, gb_ref, o_ref):
    tm = o_ref.shape[0]
    rb = min(tm, MERGE_ROW_BLOCK)
    blocks = range(0, tm, rb)
    prods = [(jnp.dot(ya_ref[r0:r0 + rb, :], wa_ref[...], preferred_element_type=F32),
              jnp.dot(yc_ref[r0:r0 + rb, :], wc_ref[...], preferred_element_type=F32)) for r0 in blocks]
    for r0, (a, c) in zip(blocks, prods):
        rows = slice(r0, r0 + rb)
        y = (jax.nn.sigmoid(ga_ref[rows, :].astype(F32)) * a
             + jax.nn.sigmoid(gb_ref[rows, :].astype(F32)) * c)
        o_ref[rows, :] = y.astype(BF16)


def _merge1(y_attn, y_conv, wa_bf, wc_bf, p_arr, *, ga_off):
    rows, ka = y_attn.shape
    kc = y_conv.shape[1]
    d = wa_bf.shape[1]
    tm = _tile(rows, 1024)
    tn = _tile(d, 512)
    o = ga_off // tn
    nb = d // tn
    return pl.pallas_call(
        _merge1_kernel,
        out_shape=jax.ShapeDtypeStruct((rows, d), BF16),
        grid=(rows // tm, nb),
        in_specs=[pl.BlockSpec((tm, ka), lambda i, j: (i, 0)),
                  pl.BlockSpec((tm, kc), lambda i, j: (i, 0)),
                  pl.BlockSpec((ka, tn), lambda i, j: (0, j)),
                  pl.BlockSpec((kc, tn), lambda i, j: (0, j)),
                  pl.BlockSpec((tm, tn), lambda i, j: (i, o + j)),
                  pl.BlockSpec((tm, tn), lambda i, j: (i, o + nb + j))],
        out_specs=pl.BlockSpec((tm, tn), lambda i, j: (i, j)),
        compiler_params=_cparams("arbitrary", "arbitrary"),
        name="merge_branches",
    )(y_attn, y_conv, wa_bf, wc_bf, p_arr, p_arr)


def _split_bf16(x):
    hi = x.astype(BF16)
    return hi, (x - hi.astype(F32)).astype(BF16)


def _merge2_kernel(y_ref, wm_ref, x_ref, m_ref, gn_ref, wr_ref, br_ref, xo_ref, hp_ref, ids_ref, tw_ref):
    tm = x_ref.shape[0]
    rb = min(tm, MIXOUT_ROW_BLOCK)
    w_hi, w_lo = _split_bf16(wr_ref[...])
    mixes = [jnp.dot(y_ref[r0:r0 + rb, :], wm_ref[...], preferred_element_type=F32) for r0 in range(0, tm, rb)]
    for r0, mix in zip(range(0, tm, rb), mixes):
        rows = slice(r0, r0 + rb)
        xn = x_ref[rows, :] + m_ref[0, 2:3, :] * mix
        xo_ref[rows, :] = xn
        h = _rms(xn) * gn_ref[...] * (1.0 + m_ref[0, 4:5, :]) + m_ref[0, 3:4, :]
        hp_ref[rows, :] = _pack_pairs(h)

        h_hi, h_lo = _split_bf16(h)
        logits = (lax.dot_general(w_hi, h_hi, NT_DIMS, preferred_element_type=F32)
                  + lax.dot_general(w_hi, h_lo, NT_DIMS, preferred_element_type=F32)
                  + lax.dot_general(w_lo, h_hi, NT_DIMS, preferred_element_type=F32)) + br_ref[...]
        n_exp = logits.shape[0]
        expert = lax.broadcasted_iota(jnp.int32, logits.shape, 0)
        vals = []
        for k in range(TOP_K):
            mx = jnp.max(logits, axis=0, keepdims=True)
            idx = jnp.min(jnp.where(logits == mx, expert, n_exp), axis=0, keepdims=True)
            ids_ref[k:k + 1, rows] = idx
            vals.append(mx)
            logits = jnp.where(expert == idx, -jnp.inf, logits)
        es = [jnp.exp(v - vals[0]) for v in vals]
        inv = 1.0 / functools.reduce(jnp.add, es)
        for k in range(TOP_K):
            tw_ref[k:k + 1, rows] = es[k] * inv


def _merge2(y, wm_bf, x2, mods, mod_row, norm_g, w_router_t, b_router, *, seq):
    rows, d = x2.shape
    n_exp = w_router_t.shape[0]
    tm = _tile(seq, 512)
    tiles_per_seq = seq // tm
    return pl.pallas_call(
        _merge2_kernel,
        out_shape=(jax.ShapeDtypeStruct((rows, d), F32),
                   jax.ShapeDtypeStruct((rows, d // 2), jnp.uint32),
                   jax.ShapeDtypeStruct((TOP_K, rows), jnp.int32),
                   jax.ShapeDtypeStruct((TOP_K, rows), F32)),
        grid=(rows // tm,),
        in_specs=[pl.BlockSpec((tm, d), lambda i: (i, 0)),
                  pl.BlockSpec((d, d), lambda i: (0, 0)),
                  pl.BlockSpec((tm, d), lambda i: (i, 0)),
                  pl.BlockSpec((1, N_MOD, d), lambda i: (mod_row(i // tiles_per_seq), 0, 0)),
                  pl.BlockSpec((1, d), lambda i: (0, 0)),
                  pl.BlockSpec((n_exp, d), lambda i: (0, 0)),
                  pl.BlockSpec((n_exp, 1), lambda i: (0, 0))],
        out_specs=(pl.BlockSpec((tm, d), lambda i: (i, 0)),
                   pl.BlockSpec((tm, d // 2), lambda i: (i, 0)),
                   pl.BlockSpec((TOP_K, tm), lambda i: (0, i)),
                   pl.BlockSpec((TOP_K, tm), lambda i: (0, i))),
        compiler_params=_cparams("arbitrary"),
        name="mixout_router",
    )(y, wm_bf, x2, mods, norm_g, w_router_t, b_router)


def _plan_kernel(ids_ref, dest_ref, te_ref, nv_ref, cnt_ref, run_ref, gs_ref, *, n_exp, tm_e):
    p = pl.program_id(0)
    t = pl.program_id(1)
    ids = ids_ref[...]
    tt = ids.shape[1]
    expert = lax.broadcasted_iota(jnp.int32, (n_exp, tt), 0)
    hit = [ids[k:k + 1, :] == expert for k in range(TOP_K)]
    member = functools.reduce(jnp.add, [h.astype(F32) for h in hit])
    tile_count = jnp.sum(member, axis=1, keepdims=True)

    @pl.when((p == 0) & (t == 0))
    def _():
        cnt_ref[...] = jnp.zeros_like(cnt_ref)

    @pl.when(p == 0)
    def _():
        cnt_ref[...] += tile_count

    @pl.when((p == 1) & (t == 0))
    def _():
        cnt = cnt_ref[...]
        ptiles = jnp.floor((cnt + (tm_e - 1)) * (1.0 / tm_e))
        r = lax.broadcasted_iota(jnp.int32, (n_exp, n_exp), 0)
        cidx = lax.broadcasted_iota(jnp.int32, (n_exp, n_exp), 1)
        lower = (cidx < r).astype(BF16)
        p_hi = jnp.floor(ptiles * (1.0 / 16.0))
        p_lo = ptiles - 16.0 * p_hi
        gstart = (16.0 * jnp.dot(lower, p_hi.astype(BF16), preferred_element_type=F32)
                  + jnp.dot(lower, p_lo.astype(BF16), preferred_element_type=F32))
        gs_ref[...] = gstart
        run_ref[...] = jnp.zeros_like(run_ref)
        ntp = te_ref.shape[1]
        tile = lax.broadcasted_iota(jnp.int32, (n_exp, ntp), 1).astype(F32)
        e2 = lax.broadcasted_iota(jnp.int32, (n_exp, ntp), 0).astype(F32)
        gend = gstart[:, 0:1] + ptiles[:, 0:1]
        te = jnp.sum((gend <= tile).astype(F32), axis=0, keepdims=True)
        used = te < n_exp
        te = jnp.minimum(te, n_exp - 1.0)
        onehot = (e2 == te).astype(F32)
        start_e = jnp.sum(onehot * gstart[:, 0:1], axis=0, keepdims=True)
        cnt_e = jnp.sum(onehot * cnt[:, 0:1], axis=0, keepdims=True)
        nv = jnp.clip(cnt_e - (tile[0:1] - start_e) * tm_e, 0.0, float(tm_e))
        te_ref[...] = te.astype(jnp.int32)
        nv_ref[...] = jnp.where(used, nv, 0.0).astype(jnp.int32)

    @pl.when(p == 1)
    def _():
        rr = lax.broadcasted_iota(jnp.int32, (tt, tt), 0)
        cc = lax.broadcasted_iota(jnp.int32, (tt, tt), 1)
        upper = (rr < cc).astype(BF16)
        before = jnp.dot(member.astype(BF16), upper, preferred_element_type=F32)
        pos = gs_ref[:, 0:1] * tm_e + run_ref[:, 0:1] + before
        for k in range(TOP_K):
            d = jnp.sum(jnp.where(hit[k], pos, 0.0), axis=0, keepdims=True)
            dest_ref[k:k + 1, :] = d.astype(jnp.int32)
        run_ref[...] += tile_count


def _plan(ids, *, n_exp, tm_e, n_tiles_pad):
    n_tok = ids.shape[1]
    tt = _tile(n_tok, 512)
    kern = functools.partial(_plan_kernel, n_exp=n_exp, tm_e=tm_e)
    return pl.pallas_call(
        kern,
        out_shape=(jax.ShapeDtypeStruct((TOP_K, n_tok), jnp.int32),
                   jax.ShapeDtypeStruct((1, n_tiles_pad), jnp.int32),
                   jax.ShapeDtypeStruct((1, n_tiles_pad), jnp.int32)),
        grid=(2, n_tok // tt),
        in_specs=[pl.BlockSpec((TOP_K, tt), lambda p, t: (0, t))],
        out_specs=(pl.BlockSpec((TOP_K, tt), lambda p, t: (0, p * t)),
                   pl.BlockSpec((1, n_tiles_pad), lambda p, t: (0, 0)),
                   pl.BlockSpec((1, n_tiles_pad), lambda p, t: (0, 0))),
        scratch_shapes=[pltpu.VMEM((n_exp, LANES), F32)] * 3,
        compiler_params=_cparams("arbitrary", "arbitrary"),
        name="route_plan",
    )(ids)


def _dest_tiles(dest, tt):
    k, n = dest.shape
    return dest.T.reshape(n // tt, 1, tt * k)


def _dispatch_body(dest_ref, h_ref, hg_ref, sem):
    groups = h_ref.shape[0]

    def issue(g, carry):
        for s in range(SUBLANES):
            for k in range(TOP_K):
                d = dest_ref[0, 0, (g * SUBLANES + s) * TOP_K + k]
                pltpu.make_async_copy(h_ref.at[g, pl.ds(s, 1)], hg_ref.at[pl.ds(d, 1)],
                                      sem.at[k % 2]).start(priority=k % 2)
        return carry

    lax.fori_loop(0, groups, issue, 0)
    for p in range(2):
        for _ in range(TOP_K // 2):
            pltpu.make_async_copy(h_ref, h_ref, sem.at[p]).wait()


def _dispatch_first_kernel(dest_ref, h_ref, hg_ref, sem):
    _dispatch_body(dest_ref, h_ref, hg_ref, sem)


def _dispatch_next_kernel(dest_ref, h_ref, hg_in_ref, hg_ref, sem):
    del hg_in_ref
    _dispatch_body(dest_ref, h_ref, hg_ref, sem)


def _dispatch(dest, hp, hg, *, n_rows_pad):
    n_tok, dp = hp.shape
    tt = _tile(n_tok, 512)
    dest_t = _dest_tiles(dest, tt)
    in_specs = [pl.BlockSpec((1, 1, tt * TOP_K), lambda i: (i, 0, 0), memory_space=pltpu.SMEM),
                pl.BlockSpec((tt // SUBLANES, SUBLANES, dp), lambda i: (i, 0, 0))]
    args = [dest_t, hp.reshape(n_tok // SUBLANES, SUBLANES, dp)]
    aliases = {}
    kern = _dispatch_first_kernel
    if hg is not None:
        in_specs.append(pl.BlockSpec(memory_space=pl.ANY))
        args.append(hg)
        aliases = {2: 0}
        kern = _dispatch_next_kernel
    return pl.pallas_call(
        kern,
        out_shape=jax.ShapeDtypeStruct((n_rows_pad, dp), jnp.uint32),
        grid=(n_tok // tt,),
        in_specs=in_specs,
        out_specs=pl.BlockSpec(memory_space=pl.ANY),
        scratch_shapes=[pltpu.SemaphoreType.DMA((2,))],
        input_output_aliases=aliases,
        compiler_params=pltpu.CompilerParams(dimension_semantics=("arbitrary",),
                                             vmem_limit_bytes=VMEM_LIMIT_BYTES, has_side_effects=True),
        name="moe_dispatch",
    )(*args)


def _mlp_kernel(te_ref, nv_ref, hg_ref, wgu_ref, bgu_ref, wd_ref, bd_ref, o_ref):
    del te_ref
    nv = nv_ref[pl.program_id(0)]

    @pl.when(nv == 0)
    def _():
        o_ref[...] = jnp.zeros_like(o_ref)

    @pl.when(nv > 0)
    def _():
        f = wd_ref.shape[1]
        packed = hg_ref[...]
        row = lax.broadcasted_iota(jnp.int32, packed.shape, 0)
        packed = jnp.where(row < nv, packed, jnp.uint32(0))
        h = _unpack_pairs_f32(packed).astype(BF16)
        gu = jnp.dot(h, wgu_ref[0], preferred_element_type=F32) + bgu_ref[0]
        gate = jnp.minimum(gu[:, :f], SWIGLU_LIMIT)
        lin = jnp.clip(gu[:, f:], -SWIGLU_LIMIT, SWIGLU_LIMIT)
        act = gate * jax.nn.sigmoid(SWIGLU_ALPHA * gate) * (lin + 1.0)
        eo = jnp.dot(act.astype(BF16), wd_ref[0], preferred_element_type=F32) + bd_ref[0]
        o_ref[...] = _pack_pairs(eo)


def _expert_mlp(te, nv, hg, wgu_bf, bgu, wd_bf, bd, *, tm_e, n_tiles):
    n_exp, d, f2 = wgu_bf.shape
    f = wd_bf.shape[1]
    dp = hg.shape[1]
    grid_spec = pltpu.PrefetchScalarGridSpec(
        num_scalar_prefetch=2,
        grid=(n_tiles,),
        in_specs=[pl.BlockSpec((tm_e, dp), lambda i, te, nv: (i, 0)),
                  pl.BlockSpec((1, d, f2), lambda i, te, nv: (te[i], 0, 0)),
                  pl.BlockSpec((1, 1, f2), lambda i, te, nv: (te[i], 0, 0)),
                  pl.BlockSpec((1, f, d), lambda i, te, nv: (te[i], 0, 0)),
                  pl.BlockSpec((1, 1, d), lambda i, te, nv: (te[i], 0, 0))],
        out_specs=pl.BlockSpec((tm_e, dp), lambda i, te, nv: (i, 0)),
    )
    return pl.pallas_call(
        _mlp_kernel,
        out_shape=jax.ShapeDtypeStruct(hg.shape, jnp.uint32),
        grid_spec=grid_spec,
        compiler_params=_cparams("arbitrary"),
        name="expert_mlp",
    )(te, nv, hg, wgu_bf, bgu.reshape(n_exp, 1, f2), wd_bf, bd.reshape(n_exp, 1, d))


def _combine_kernel(dest_ref, dest_next_ref, tw_ref, x_ref, m_ref, eo_ref, o_ref, g_ref, sem):
    tt = x_ref.shape[0]
    groups = tt // SUBLANES
    dp = g_ref.shape[-1]
    i = pl.program_id(0)
    slot = i % 2

    def issue_all(idx_ref, sl):
        def issue(g, carry):
            for s in range(SUBLANES):
                for k in range(TOP_K):
                    d = idx_ref[0, 0, (g * SUBLANES + s) * TOP_K + k]
                    pltpu.make_async_copy(eo_ref.at[pl.ds(d, 1)], g_ref.at[sl, k, g, pl.ds(s, 1)],
                                          sem.at[sl]).start(priority=k % 2)
            return carry
        lax.fori_loop(0, groups, issue, 0)

    @pl.when(i == 0)
    def _():
        issue_all(dest_ref, 0)

    @pl.when(i + 1 < pl.num_programs(0))
    def _():
        issue_all(dest_next_ref, 1 - slot)

    pltpu.make_async_copy(g_ref.at[slot], g_ref.at[slot], sem.at[slot]).wait()
    y = None
    for k in range(TOP_K):
        part = tw_ref[:, k:k + 1] * _unpack_pairs_f32(g_ref[slot, k].reshape(tt, dp))
        y = part if y is None else y + part
    o_ref[...] = x_ref[...] + m_ref[0, 5:6, :] * y


def _combine(dest, tw_t, x2, mods, mod_row, eo, *, seq):
    n_tok, d = x2.shape
    dp = eo.shape[1]
    tt = _tile(seq, 256)
    tiles_per_seq = seq // tt
    dest_t = _dest_tiles(dest, tt)
    n_steps = n_tok // tt
    return pl.pallas_call(
        _combine_kernel,
        out_shape=jax.ShapeDtypeStruct((n_tok, d), F32),
        grid=(n_steps,),
        in_specs=[pl.BlockSpec((1, 1, tt * TOP_K), lambda i: (i, 0, 0), memory_space=pltpu.SMEM),
                  pl.BlockSpec((1, 1, tt * TOP_K), lambda i: (jnp.minimum(i + 1, n_steps - 1), 0, 0),
                               memory_space=pltpu.SMEM),
                  pl.BlockSpec((tt, TOP_K), lambda i: (i, 0)),
                  pl.BlockSpec((tt, d), lambda i: (i, 0)),
                  pl.BlockSpec((1, N_MOD, d), lambda i: (mod_row(i // tiles_per_seq), 0, 0)),
                  pl.BlockSpec(memory_space=pl.ANY)],
        out_specs=pl.BlockSpec((tt, d), lambda i: (i, 0)),
        scratch_shapes=[pltpu.VMEM((2, TOP_K, tt // SUBLANES, SUBLANES, dp), jnp.uint32),
                        pltpu.SemaphoreType.DMA((2,))],
        compiler_params=_cparams("arbitrary"),
        name="moe_combine",
    )(dest_t, dest_t, tw_t, x2, mods, eo)


def _rope_tables(seq, hd):
    n_freq = hd // 4
    pos = jnp.arange(seq)
    row = (pos // GRID_W).astype(F32)
    col = (pos % GRID_W).astype(F32)
    inv = ROPE_BASE ** (-jnp.arange(n_freq, dtype=F32) / n_freq)
    ang = jnp.concatenate([row[:, None] * inv, col[:, None] * inv], axis=1)
    ang = jnp.concatenate([ang, ang], axis=1)
    sign = jnp.where(jnp.arange(hd) < hd // 2, -1.0, 1.0)
    return jnp.cos(ang), sign * jnp.sin(ang)


def _half_major(a, hd):
    lead = a.shape[:-1]
    a = a.reshape(*lead, a.shape[-1] // hd, 2, 2, hd // 4)
    return jnp.swapaxes(a, -3, -2).reshape(*lead, -1)


def kernel(x, c, ctx, c_ctx, w_ada, b_ada, norm_mix_g, norm_ffn_g, w_in, q_norm_g, k_norm_g, lambda_qk,
           subln_g, conv_w, w_attn_out, w_conv_out, w_mix_out, w_router, b_router, w_gate_up, b_gate_up,
           w_down, b_down):
    batch, seq, d = x.shape
    n_ctx = ctx.shape[1]
    depth = w_ada.shape[0]
    hd = q_norm_g.shape[-1]
    n_cols = w_in.shape[-1]
    qkw = (n_cols - 5 * d) // 3
    n_heads = qkw // (2 * hd)
    n_exp = w_router.shape[-1]
    u_off = 3 * qkw
    ga_off = u_off + 3 * d

    cos, sin_s = _rope_tables(seq, hd)
    n_mod_rows = -(-(batch + 1) // 8) * 8
    cv = jnp.zeros((n_mod_rows, d), F32).at[:batch].set(c).at[batch].set(c_ctx)
    mods_all = _adaln(cv, w_ada, b_ada).reshape(depth, n_mod_rows, N_MOD, d)
    lat_row = lambda b: b
    ctx_row = lambda b: batch

    xs = x.reshape(batch * seq, d)
    cs = ctx.reshape(batch * n_ctx, d)
    for l in range(depth):
        last = l == depth - 1
        lam_init = 0.8 - 0.6 * math.exp(-0.3 * l)
        mods = mods_all[l]
        g_mix = norm_mix_g[l][None]
        g_ffn = norm_ffn_g[l][None]
        q_g = _half_major(q_norm_g[l][None], hd)
        k_g = _half_major(k_norm_g[l][None], hd)
        sub_g = subln_g[l][None]
        w_in_bf = _layer_bf16(w_in, l)
        w_in_bf = w_in_bf.at[:, :2 * qkw].set(_half_major(w_in_bf[:, :2 * qkw], hd))

        p_lat = _inproj(xs, mods, lat_row, g_mix, w_in_bf, q_g, k_g, cos, sin_s,
                        qkw=qkw, n_cols=n_cols, seq=seq, rope=True, hd=hd)
        p_ctx = _inproj(cs, mods, ctx_row, g_mix, w_in_bf, q_g, k_g, cos, sin_s,
                        qkw=qkw, n_cols=(3 * qkw if last else n_cols), seq=n_ctx, rope=False, hd=hd)
        attn_args = dict(batch=batch, n_heads=n_heads, hd=hd, qkw=qkw, lam_init=lam_init)
        y_attn = _attention(lambda_qk[l], sub_g, p_lat, [p_ctx, p_lat], q_len=seq,
                            kv_lens=[n_ctx, seq], **attn_args)
        y_conv = _short_conv(p_lat, conv_w[l], batch=batch, seq=seq, u_off=u_off, width=d)
        wa_bf = _layer_bf16(w_attn_out, l)
        wc_bf = _layer_bf16(w_conv_out, l)
        wm_bf = _layer_bf16(w_mix_out, l)
        w_router_t = w_router[l].T
        b_router_c = b_router[l][:, None]
        y_lat = _merge1(y_attn, y_conv, wa_bf, wc_bf, p_lat, ga_off=ga_off)
        xs, hp_lat, ids_lat, tw_lat = _merge2(y_lat, wm_bf, xs, mods, lat_row, g_ffn, w_router_t, b_router_c,
                                              seq=seq)
        if not last:
            y_attn_c = _attention(lambda_qk[l], sub_g, p_ctx, [p_ctx], q_len=n_ctx, kv_lens=[n_ctx],
                                  **attn_args)
            y_conv_c = _short_conv(p_ctx, conv_w[l], batch=batch, seq=n_ctx, u_off=u_off, width=d)
            y_ctx = _merge1(y_attn_c, y_conv_c, wa_bf, wc_bf, p_ctx, ga_off=ga_off)
            cs, hp_ctx, ids_ctx, tw_ctx = _merge2(y_ctx, wm_bf, cs, mods, ctx_row, g_ffn, w_router_t,
                                                  b_router_c, seq=n_ctx)
            ids = jnp.concatenate([ids_ctx, ids_lat], axis=1)
        else:
            ids = ids_lat

        n_tok = ids.shape[1]
        n_pairs = n_tok * TOP_K
        tm_e = 512 if n_pairs >= 32768 else 32
        n_tiles = n_pairs // tm_e + n_exp
        n_tiles_pad = -(-n_tiles // LANES) * LANES
        dest, te, nv = _plan(ids, n_exp=n_exp, tm_e=tm_e, n_tiles_pad=n_tiles_pad)
        n_rows_pad = n_tiles * tm_e
        if not last:
            n_c = batch * n_ctx
            hg = _dispatch(dest[:, :n_c], hp_ctx, None, n_rows_pad=n_rows_pad)
            hg = _dispatch(dest[:, n_c:], hp_lat, hg, n_rows_pad=n_rows_pad)
        else:
            hg = _dispatch(dest, hp_lat, None, n_rows_pad=n_rows_pad)
        eo = _expert_mlp(te[0], nv[0], hg, _layer_bf16(w_gate_up, l), b_gate_up[l],
                         _layer_bf16(w_down, l), b_down[l], tm_e=tm_e, n_tiles=n_tiles)
        if not last:
            cs = _combine(dest[:, :n_c], tw_ctx.T, cs, mods, ctx_row, eo, seq=n_ctx)
            xs = _combine(dest[:, n_c:], tw_lat.T, xs, mods, lat_row, eo, seq=seq)
        else:
            xs = _combine(dest, tw_lat.T, xs, mods, lat_row, eo, seq=seq)
    return xs.reshape(batch, seq, d)
```

```python
import functools
import math

import jax
import jax.numpy as jnp
from jax import lax
from jax.experimental import pallas as pl
from jax.experimental.pallas import tpu as pltpu

GRID_W = 64
TOP_K = 4
ROPE_BASE = 10000.0
RMS_EPS = 1e-6
SWIGLU_LIMIT = 7.0
SWIGLU_ALPHA = 1.702
N_MOD = 6
LANES = 128
SUBLANES = 8
CAST_BLOCK_ELEMS = 2 ** 21
INPROJ_ROW_BLOCK = 256
MERGE_ROW_BLOCK = 256
MIXOUT_ROW_BLOCK = 128
ATTN_ROW_BLOCK = 128
ATTN_KEY_CHUNK = 256
ATTN_SCORE_SLOTS = 8
VMEM_LIMIT_BYTES = 56 * 2 ** 20

F32 = jnp.float32
BF16 = jnp.bfloat16
NT_DIMS = (((1,), (1,)), ((), ()))


def _cparams(*sem):
    return pltpu.CompilerParams(dimension_semantics=sem, vmem_limit_bytes=VMEM_LIMIT_BYTES)


def _tile(n, pref):
    if n <= pref:
        return n
    for t in range(pref, 7, -1):
        if n % t == 0 and t % 8 == 0:
            return t
    return n


def _pack_pairs(x):
    n = x.shape[1] // 2
    hi = lax.bitcast_convert_type(x[:, :n].astype(BF16).astype(F32), jnp.uint32)
    lo = lax.bitcast_convert_type(x[:, n:].astype(BF16).astype(F32), jnp.uint32)
    return hi | (lo >> 16)


def _unpack_pairs_f32(p):
    left = lax.bitcast_convert_type(p & jnp.uint32(0xFFFF0000), F32)
    right = lax.bitcast_convert_type(p << 16, F32)
    return jnp.concatenate([left, right], axis=1)


def _rms(x):
    return x * lax.rsqrt(jnp.mean(x * x, axis=-1, keepdims=True) + RMS_EPS)


def _cast_kernel(w_ref, o_ref):
    o_ref[...] = w_ref[0].astype(BF16)


def _layer_bf16(w, layer):
    cols = w.shape[-1]
    w3 = w.reshape(w.shape[0], -1, cols)
    rows = w3.shape[1]
    tr = _tile(rows, max(8, CAST_BLOCK_ELEMS // cols))
    out = pl.pallas_call(
        _cast_kernel,
        out_shape=jax.ShapeDtypeStruct((rows, cols), BF16),
        grid=(rows // tr,),
        in_specs=[pl.BlockSpec((1, tr, cols), lambda i: (layer, i, 0))],
        out_specs=pl.BlockSpec((tr, cols), lambda i: (i, 0)),
        compiler_params=_cparams("arbitrary"),
        name="cast_bf16",
    )(w3)
    return out.reshape(w.shape[1:])


def _adaln_kernel(cv_ref, w_ref, b_ref, o_ref):
    cv = cv_ref[...]
    a = (cv * jax.nn.sigmoid(cv)).astype(BF16)
    o_ref[0] = jnp.dot(a, w_ref[0].astype(BF16), preferred_element_type=F32) + b_ref[0]


def _adaln(cv, w_ada, b_ada):
    n_layers, d, n = w_ada.shape
    r = cv.shape[0]
    tn = _tile(n, 1024)
    return pl.pallas_call(
        _adaln_kernel,
        out_shape=jax.ShapeDtypeStruct((n_layers, r, n), F32),
        grid=(n_layers, n // tn),
        in_specs=[pl.BlockSpec((r, d), lambda l, j: (0, 0)),
                  pl.BlockSpec((1, d, tn), lambda l, j: (l, 0, j)),
                  pl.BlockSpec((1, 1, tn), lambda l, j: (l, 0, j))],
        out_specs=pl.BlockSpec((1, r, tn), lambda l, j: (l, 0, j)),
        compiler_params=_cparams("arbitrary", "arbitrary"),
        name="adaln",
    )(cv, w_ada, b_ada.reshape(n_layers, 1, n))


def _inproj_kernel(x_ref, m_ref, g_ref, w_ref, qg_ref, kg_ref, cos_ref, sin_ref, o_ref, h_ref,
                   *, nq, rope, hd, q_scale):
    j = pl.program_id(1)

    tm, tn = o_ref.shape
    rb = min(tm, INPROJ_ROW_BLOCK)

    def qk_step(gain, first):
        if first:
            for r0 in range(0, tm, rb):
                h = _rms(x_ref[r0:r0 + rb, :]) * g_ref[...] * (1.0 + m_ref[0, 1:2, :]) + m_ref[0, 0:1, :]
                h_ref[r0:r0 + rb, :] = h.astype(BF16)
        accs = [jnp.dot(h_ref[r0:r0 + rb, :], w_ref[...], preferred_element_type=F32) for r0 in range(0, tm, rb)]
        for r0, acc in zip(range(0, tm, rb), accs):
            for g in range(0, tn, hd):
                n = _rms(acc[:, g:g + hd]) * gain
                if rope:
                    n = (n * cos_ref[r0:r0 + rb, :]
                         + pltpu.roll(n, hd // 2, axis=1) * sin_ref[r0:r0 + rb, :])
                o_ref[r0:r0 + rb, g:g + hd] = n.astype(BF16)

    @pl.when(j == 0)
    def _():
        qk_step(qg_ref[...] * q_scale, True)

    @pl.when((j > 0) & (j < nq))
    def _():
        qk_step(qg_ref[...] * q_scale, False)

    @pl.when((j >= nq) & (j < 2 * nq))
    def _():
        qk_step(kg_ref[...], False)

    @pl.when(j >= 2 * nq)
    def _():
        o_ref[...] = jnp.dot(h_ref[...], w_ref[...], preferred_element_type=F32).astype(BF16)


def _inproj(x2, mods, mod_row, norm_g, w_bf, q_g, k_g, cos, sin_s, *, qkw, n_cols, seq, rope, hd):
    rows, d = x2.shape
    tm = _tile(seq, 1024)
    tn = _tile(qkw, 1024)
    tiles_per_seq = seq // tm
    q_scale = float(hd) ** -0.5 * math.log2(math.e)
    kern = functools.partial(_inproj_kernel, nq=qkw // tn, rope=rope, hd=hd, q_scale=q_scale)
    return pl.pallas_call(
        kern,
        out_shape=jax.ShapeDtypeStruct((rows, n_cols), BF16),
        grid=(rows // tm, n_cols // tn),
        in_specs=[pl.BlockSpec((tm, d), lambda i, j: (i, 0)),
                  pl.BlockSpec((1, N_MOD, d), lambda i, j: (mod_row(i // tiles_per_seq), 0, 0)),
                  pl.BlockSpec((1, d), lambda i, j: (0, 0)),
                  pl.BlockSpec((d, tn), lambda i, j: (0, j)),
                  pl.BlockSpec((1, hd), lambda i, j: (0, 0)),
                  pl.BlockSpec((1, hd), lambda i, j: (0, 0)),
                  pl.BlockSpec((tm, hd), lambda i, j: ((i % tiles_per_seq) if rope else 0, 0)),
                  pl.BlockSpec((tm, hd), lambda i, j: ((i % tiles_per_seq) if rope else 0, 0))],
        out_specs=pl.BlockSpec((tm, tn), lambda i, j: (i, j)),
        scratch_shapes=[pltpu.VMEM((tm, d), BF16)],
        compiler_params=_cparams("arbitrary", "arbitrary"),
        name="inproj",
    )(x2, mods, norm_g, w_bf, q_g, k_g, cos, sin_s)


def _attn_kernel(lq_ref, sg_ref, q_ref, *refs, n_src, n_ride, lam_init, hd):
    k_refs = refs[0:2 * n_src:2]
    v_refs = refs[1:2 * n_src:2]
    ride_in = refs[2 * n_src:2 * n_src + n_ride]
    o_ref = refs[2 * n_src + n_ride]
    ride_out = refs[2 * n_src + n_ride + 1:2 * n_src + 2 * n_ride + 1]
    s_ref = refs[-1]
    for w_in_ref, w_out_ref in zip(ride_in, ride_out):
        w_out_ref[...] = w_in_ref[0].astype(BF16)
    lq = lq_ref[...]
    lam = (jnp.exp(jnp.sum(lq[0:1] * lq[1:2], axis=1, keepdims=True))
           - jnp.exp(jnp.sum(lq[2:3] * lq[3:4], axis=1, keepdims=True)) + lam_init)
    tq = q_ref.shape[0]
    rb = min(tq, ATTN_ROW_BLOCK)
    chunks = []
    col = 0
    for si in range(n_src):
        n = k_refs[si].shape[0]
        for st in range(0, n, ATTN_KEY_CHUNK):
            size = min(ATTN_KEY_CHUNK, n - st)
            chunks.append((si, st, size, col))
            col += size

    def lane_tiles(size):
        step = LANES if size % LANES == 0 else size
        return [(j, step) for j in range(0, size, step)]

    class RowReduce:
        def __init__(self, op, combine):
            self.op, self.combine, self.wide, self.narrow = op, combine, None, None

        def add(self, part):
            if part.shape[1] == LANES:
                self.wide = part if self.wide is None else self.combine(self.wide, part)
            else:
                part = self.op(part, axis=1, keepdims=True)
                self.narrow = part if self.narrow is None else self.combine(self.narrow, part)

        def result(self):
            out = [] if self.narrow is None else [self.narrow]
            if self.wide is not None:
                out.append(self.op(self.wide, axis=1, keepdims=True))
            return functools.reduce(self.combine, out)

    for r0 in range(0, tq, rb):
        maps = []
        for m in range(2):
            qm = q_ref[r0:r0 + rb, m * hd:(m + 1) * hd]
            slot = (2 * (r0 // rb) + m) % s_ref.shape[0]
            row_max = RowReduce(jnp.max, jnp.maximum)
            for si, st, size, col in chunks:
                s = lax.dot_general(qm, k_refs[si][st:st + size, m * hd:(m + 1) * hd], NT_DIMS,
                                    preferred_element_type=F32)
                s_ref[slot, :, col:col + size] = s
                for j, w in lane_tiles(size):
                    row_max.add(s[:, j:j + w])
            mx = row_max.result()
            mx_wide = jnp.broadcast_to(mx, (rb, LANES))
            row_sum = RowReduce(jnp.sum, jnp.add)
            pv = None
            for si, st, size, col in chunks:
                probs = []
                for j, w in lane_tiles(size):
                    e = jnp.exp2(s_ref[slot, :, col + j:col + j + w] - mx_wide[:, :w])
                    row_sum.add(e)
                    probs.append(e.astype(BF16))
                p = probs[0] if len(probs) == 1 else jnp.concatenate(probs, axis=1)
                d = jnp.dot(p, v_refs[si][st:st + size, :], preferred_element_type=F32)
                pv = d if pv is None else pv + d
            norm = row_sum.result()
            maps.append(pv * (1.0 / norm))
        o = maps[0] - lam * maps[1]
        o = _rms(o) * sg_ref[...] * (1.0 - lam_init)
        o_ref[r0:r0 + rb, :] = o.astype(BF16)


def _attention(lq, sub_g, q_arr, kv_arrs, *, batch, q_len, kv_lens, n_heads, hd, qkw, lam_init, ride=()):
    vd = 2 * hd
    tq = _tile(q_len, 2048)
    rb = min(tq, ATTN_ROW_BLOCK)
    nqb = q_len // tq
    k_off = qkw // vd
    v_off = 2 * qkw // vd
    in_specs = [pl.BlockSpec((4, hd), lambda b, h, i: (0, 0)),
                pl.BlockSpec((1, vd), lambda b, h, i: (0, 0)),
                pl.BlockSpec((tq, vd), lambda b, h, i: (b * nqb + i, h))]
    args = [lq, sub_g, q_arr]
    for arr, n in zip(kv_arrs, kv_lens):
        in_specs.append(pl.BlockSpec((n, vd), lambda b, h, i: (b, k_off + h)))
        in_specs.append(pl.BlockSpec((n, vd), lambda b, h, i: (b, v_off + h)))
        args += [arr, arr]
    n_steps = batch * n_heads * nqb
    out_shape = [jax.ShapeDtypeStruct((batch * q_len, n_heads * vd), BF16)]
    out_specs = [pl.BlockSpec((tq, vd), lambda b, h, i: (b * nqb + i, h))]
    for w, layer in ride:
        cols = w.shape[-1]
        w3 = w.reshape(w.shape[0], -1, cols)
        tr = w3.shape[1] // n_steps
        assert tr * n_steps == w3.shape[1] and tr % 8 == 0, (w.shape, n_steps)
        step = lambda b, h, i: (b * n_heads + h) * nqb + i
        in_specs.append(pl.BlockSpec((1, tr, cols), lambda b, h, i, layer=layer: (layer, step(b, h, i), 0)))
        args.append(w3)
        out_shape.append(jax.ShapeDtypeStruct((w3.shape[1], cols), BF16))
        out_specs.append(pl.BlockSpec((tr, cols), lambda b, h, i: (step(b, h, i), 0)))
    kern = functools.partial(_attn_kernel, n_src=len(kv_arrs), n_ride=len(ride), lam_init=lam_init, hd=hd)
    outs = pl.pallas_call(
        kern,
        out_shape=out_shape,
        grid=(batch, n_heads, nqb),
        in_specs=in_specs,
        out_specs=out_specs,
        scratch_shapes=[pltpu.VMEM((min(ATTN_SCORE_SLOTS, 2 * tq // rb), rb, sum(kv_lens)), F32)],
        compiler_params=_cparams("arbitrary", "arbitrary", "arbitrary"),
        name="diff_attn",
    )(*args)
    return [outs[0]] + [o.reshape(w.shape[1:]) for o, (w, _) in zip(outs[1:], ride)]


def _conv_kernel(u_ref, b_ref, c_ref, w_ref, o_ref):
    cu = c_ref[...].astype(F32) * u_ref[...].astype(F32)
    n = cu.shape[0]
    row = lax.broadcasted_iota(jnp.int32, cu.shape, 0)
    prev = jnp.where(row == 0, 0.0, pltpu.roll(cu, 1, axis=0))
    nxt = jnp.where(row == n - 1, 0.0, pltpu.roll(cu, n - 1, axis=0))
    w = w_ref[...]
    y = b_ref[...].astype(F32) * (w[0:1] * prev + w[1:2] * cu + w[2:3] * nxt)
    o_ref[...] = y.astype(BF16)


def _short_conv(p_arr, conv_w, *, batch, seq, u_off, width):
    tc = _tile(width, 512)
    o = u_off // tc
    nb = width // tc
    return pl.pallas_call(
        _conv_kernel,
        out_shape=jax.ShapeDtypeStruct((batch * seq, width), BF16),
        grid=(batch, nb),
        in_specs=[pl.BlockSpec((seq, tc), lambda b, j: (b, o + j)),
                  pl.BlockSpec((seq, tc), lambda b, j: (b, o + nb + j)),
                  pl.BlockSpec((seq, tc), lambda b, j: (b, o + 2 * nb + j)),
                  pl.BlockSpec((conv_w.shape[0], tc), lambda b, j: (0, j))],
        out_specs=pl.BlockSpec((seq, tc), lambda b, j: (b, j)),
        compiler_params=_cparams("arbitrary", "arbitrary"),
        name="short_conv",
    )(p_arr, p_arr, p_arr, conv_w)


def _merge1_kernel(ya_ref, yc_ref, wa_ref, wc_ref, ga_ref, gb_ref, o_ref):
    tm = o_ref.shape[0]
    rb = min(tm, MERGE_ROW_BLOCK)
    blocks = range(0, tm, rb)
    prods = [(jnp.dot(ya_ref[r0:r0 + rb, :], wa_ref[...], preferred_element_type=F32),
              jnp.dot(yc_ref[r0:r0 + rb, :], wc_ref[...], preferred_element_type=F32)) for r0 in blocks]
    for r0, (a, c) in zip(blocks, prods):
        rows = slice(r0, r0 + rb)
        y = (jax.nn.sigmoid(ga_ref[rows, :].astype(F32)) * a
             + jax.nn.sigmoid(gb_ref[rows, :].astype(F32)) * c)
        o_ref[rows, :] = y.astype(BF16)


def _merge1(y_attn, y_conv, wa_bf, wc_bf, p_arr, *, ga_off):
    rows, ka = y_attn.shape
    kc = y_conv.shape[1]
    d = wa_bf.shape[1]
    tm = _tile(rows, 1024)
    tn = _tile(d, 512)
    o = ga_off // tn
    nb = d // tn
    return pl.pallas_call(
        _merge1_kernel,
        out_shape=jax.ShapeDtypeStruct((rows, d), BF16),
        grid=(rows // tm, nb),
        in_specs=[pl.BlockSpec((tm, ka), lambda i, j: (i, 0)),
                  pl.BlockSpec((tm, kc), lambda i, j: (i, 0)),
                  pl.BlockSpec((ka, tn), lambda i, j: (0, j)),
                  pl.BlockSpec((kc, tn), lambda i, j: (0, j)),
                  pl.BlockSpec((tm, tn), lambda i, j: (i, o + j)),
                  pl.BlockSpec((tm, tn), lambda i, j: (i, o + nb + j))],
        out_specs=pl.BlockSpec((tm, tn), lambda i, j: (i, j)),
        compiler_params=_cparams("arbitrary", "arbitrary"),
        name="merge_branches",
    )(y_attn, y_conv, wa_bf, wc_bf, p_arr, p_arr)


def _split_bf16(x):
    hi = x.astype(BF16)
    return hi, (x - hi.astype(F32)).astype(BF16)


def _merge2_kernel(y_ref, wm_ref, x_ref, m_ref, gn_ref, wr_ref, br_ref, xo_ref, hp_ref, ids_ref, tw_ref):
    tm = x_ref.shape[0]
    rb = min(tm, MIXOUT_ROW_BLOCK)
    w_hi, w_lo = _split_bf16(wr_ref[...])
    mixes = [jnp.dot(y_ref[r0:r0 + rb, :], wm_ref[...], preferred_element_type=F32) for r0 in range(0, tm, rb)]
    for r0, mix in zip(range(0, tm, rb), mixes):
        rows = slice(r0, r0 + rb)
        xn = x_ref[rows, :] + m_ref[0, 2:3, :] * mix
        xo_ref[rows, :] = xn
        h = _rms(xn) * gn_ref[...] * (1.0 + m_ref[0, 4:5, :]) + m_ref[0, 3:4, :]
        hp_ref[rows, :] = _pack_pairs(h)

        h_hi, h_lo = _split_bf16(h)
        logits = (lax.dot_general(w_hi, h_hi, NT_DIMS, preferred_element_type=F32)
                  + lax.dot_general(w_hi, h_lo, NT_DIMS, preferred_element_type=F32)
                  + lax.dot_general(w_lo, h_hi, NT_DIMS, preferred_element_type=F32)) + br_ref[...]
        n_exp = logits.shape[0]
        expert = lax.broadcasted_iota(jnp.int32, logits.shape, 0)
        vals = []
        for k in range(TOP_K):
            mx = jnp.max(logits, axis=0, keepdims=True)
            idx = jnp.min(jnp.where(logits == mx, expert, n_exp), axis=0, keepdims=True)
            ids_ref[k:k + 1, rows] = idx
            vals.append(mx)
            logits = jnp.where(expert == idx, -jnp.inf, logits)
        es = [jnp.exp(v - vals[0]) for v in vals]
        inv = 1.0 / functools.reduce(jnp.add, es)
        for k in range(TOP_K):
            tw_ref[k:k + 1, rows] = es[k] * inv


def _merge2(y, wm_bf, x2, mods, mod_row, norm_g, w_router_t, b_router, *, seq):
    rows, d = x2.shape
    n_exp = w_router_t.shape[0]
    tm = _tile(seq, 512)
    tiles_per_seq = seq // tm
    return pl.pallas_call(
        _merge2_kernel,
        out_shape=(jax.ShapeDtypeStruct((rows, d), F32),
                   jax.ShapeDtypeStruct((rows, d // 2), jnp.uint32),
                   jax.ShapeDtypeStruct((TOP_K, rows), jnp.int32),
                   jax.ShapeDtypeStruct((TOP_K, rows), F32)),
        grid=(rows // tm,),
        in_specs=[pl.BlockSpec((tm, d), lambda i: (i, 0)),
                  pl.BlockSpec((d, d), lambda i: (0, 0)),
                  pl.BlockSpec((tm, d), lambda i: (i, 0)),
                  pl.BlockSpec((1, N_MOD, d), lambda i: (mod_row(i // tiles_per_seq), 0, 0)),
                  pl.BlockSpec((1, d), lambda i: (0, 0)),
                  pl.BlockSpec((n_exp, d), lambda i: (0, 0)),
                  pl.BlockSpec((n_exp, 1), lambda i: (0, 0))],
        out_specs=(pl.BlockSpec((tm, d), lambda i: (i, 0)),
                   pl.BlockSpec((tm, d // 2), lambda i: (i, 0)),
                   pl.BlockSpec((TOP_K, tm), lambda i: (0, i)),
                   pl.BlockSpec((TOP_K, tm), lambda i: (0, i))),
        compiler_params=_cparams("arbitrary"),
        name="mixout_router",
    )(y, wm_bf, x2, mods, norm_g, w_router_t, b_router)


def _plan_kernel(ids_ref, dest_ref, te_ref, nv_ref, cnt_ref, run_ref, gs_ref, *, n_exp, tm_e):
    p = pl.program_id(0)
    t = pl.program_id(1)
    ids = ids_ref[...]
    tt = ids.shape[1]
    expert = lax.broadcasted_iota(jnp.int32, (n_exp, tt), 0)
    hit = [ids[k:k + 1, :] == expert for k in range(TOP_K)]
    member = functools.reduce(jnp.add, [h.astype(F32) for h in hit])
    tile_count = jnp.sum(member, axis=1, keepdims=True)

    @pl.when((p == 0) & (t == 0))
    def _():
        cnt_ref[...] = jnp.zeros_like(cnt_ref)

    @pl.when(p == 0)
    def _():
        cnt_ref[...] += tile_count

    @pl.when((p == 1) & (t == 0))
    def _():
        cnt = cnt_ref[...]
        ptiles = jnp.floor((cnt + (tm_e - 1)) * (1.0 / tm_e))
        r = lax.broadcasted_iota(jnp.int32, (n_exp, n_exp), 0)
        cidx = lax.broadcasted_iota(jnp.int32, (n_exp, n_exp), 1)
        lower = (cidx < r).astype(BF16)
        p_hi = jnp.floor(ptiles * (1.0 / 16.0))
        p_lo = ptiles - 16.0 * p_hi
        gstart = (16.0 * jnp.dot(lower, p_hi.astype(BF16), preferred_element_type=F32)
                  + jnp.dot(lower, p_lo.astype(BF16), preferred_element_type=F32))
        gs_ref[...] = gstart
        run_ref[...] = jnp.zeros_like(run_ref)
        ntp = te_ref.shape[1]
        tile = lax.broadcasted_iota(jnp.int32, (n_exp, ntp), 1).astype(F32)
        e2 = lax.broadcasted_iota(jnp.int32, (n_exp, ntp), 0).astype(F32)
        gend = gstart[:, 0:1] + ptiles[:, 0:1]
        te = jnp.sum((gend <= tile).astype(F32), axis=0, keepdims=True)
        used = te < n_exp
        te = jnp.minimum(te, n_exp - 1.0)
        onehot = (e2 == te).astype(F32)
        start_e = jnp.sum(onehot * gstart[:, 0:1], axis=0, keepdims=True)
        cnt_e = jnp.sum(onehot * cnt[:, 0:1], axis=0, keepdims=True)
        nv = jnp.clip(cnt_e - (tile[0:1] - start_e) * tm_e, 0.0, float(tm_e))
        te_ref[...] = te.astype(jnp.int32)
        nv_ref[...] = jnp.where(used, nv, 0.0).astype(jnp.int32)

    @pl.when(p == 1)
    def _():
        rr = lax.broadcasted_iota(jnp.int32, (tt, tt), 0)
        cc = lax.broadcasted_iota(jnp.int32, (tt, tt), 1)
        upper = (rr < cc).astype(BF16)
        before = jnp.dot(member.astype(BF16), upper, preferred_element_type=F32)
        pos = gs_ref[:, 0:1] * tm_e + run_ref[:, 0:1] + before
        for k in range(TOP_K):
            d = jnp.sum(jnp.where(hit[k], pos, 0.0), axis=0, keepdims=True)
            dest_ref[k:k + 1, :] = d.astype(jnp.int32)
        run_ref[...] += tile_count


def _plan(ids, *, n_exp, tm_e, n_tiles_pad):
    n_tok = ids.shape[1]
    tt = _tile(n_tok, 512)
    kern = functools.partial(_plan_kernel, n_exp=n_exp, tm_e=tm_e)
    return pl.pallas_call(
        kern,
        out_shape=(jax.ShapeDtypeStruct((TOP_K, n_tok), jnp.int32),
                   jax.ShapeDtypeStruct((1, n_tiles_pad), jnp.int32),
                   jax.ShapeDtypeStruct((1, n_tiles_pad), jnp.int32)),
        grid=(2, n_tok // tt),
        in_specs=[pl.BlockSpec((TOP_K, tt), lambda p, t: (0, t))],
        out_specs=(pl.BlockSpec((TOP_K, tt), lambda p, t: (0, p * t)),
                   pl.BlockSpec((1, n_tiles_pad), lambda p, t: (0, 0)),
                   pl.BlockSpec((1, n_tiles_pad), lambda p, t: (0, 0))),
        scratch_shapes=[pltpu.VMEM((n_exp, LANES), F32)] * 3,
        compiler_params=_cparams("arbitrary", "arbitrary"),
        name="route_plan",
    )(ids)


def _dest_tiles(dest, tt):
    k, n = dest.shape
    return dest.T.reshape(n // tt, 1, tt * k)


def _dispatch_body(dest_ref, h_ref, hg_ref, sem):
    groups = h_ref.shape[0]

    def issue(g, carry):
        for s in range(SUBLANES):
            for k in range(TOP_K):
                d = dest_ref[0, 0, (g * SUBLANES + s) * TOP_K + k]
                pltpu.make_async_copy(h_ref.at[g, pl.ds(s, 1)], hg_ref.at[pl.ds(d, 1)],
                                      sem.at[k % 2]).start(priority=k % 2)
        return carry

    lax.fori_loop(0, groups, issue, 0)
    for p in range(2):
        for _ in range(TOP_K // 2):
            pltpu.make_async_copy(h_ref, h_ref, sem.at[p]).wait()


def _dispatch_first_kernel(dest_ref, h_ref, hg_ref, sem):
    _dispatch_body(dest_ref, h_ref, hg_ref, sem)


def _dispatch_next_kernel(dest_ref, h_ref, hg_in_ref, hg_ref, sem):
    del hg_in_ref
    _dispatch_body(dest_ref, h_ref, hg_ref, sem)


def _dispatch(dest, hp, hg, *, n_rows_pad):
    n_tok, dp = hp.shape
    tt = _tile(n_tok, 1024)
    dest_t = _dest_tiles(dest, tt)
    in_specs = [pl.BlockSpec((1, 1, tt * TOP_K), lambda i: (i, 0, 0), memory_space=pltpu.SMEM),
                pl.BlockSpec((tt // SUBLANES, SUBLANES, dp), lambda i: (i, 0, 0))]
    args = [dest_t, hp.reshape(n_tok // SUBLANES, SUBLANES, dp)]
    aliases = {}
    kern = _dispatch_first_kernel
    if hg is not None:
        in_specs.append(pl.BlockSpec(memory_space=pl.ANY))
        args.append(hg)
        aliases = {2: 0}
        kern = _dispatch_next_kernel
    return pl.pallas_call(
        kern,
        out_shape=jax.ShapeDtypeStruct((n_rows_pad, dp), jnp.uint32),
        grid=(n_tok // tt,),
        in_specs=in_specs,
        out_specs=pl.BlockSpec(memory_space=pl.ANY),
        scratch_shapes=[pltpu.SemaphoreType.DMA((2,))],
        input_output_aliases=aliases,
        compiler_params=pltpu.CompilerParams(dimension_semantics=("arbitrary",),
                                             vmem_limit_bytes=VMEM_LIMIT_BYTES, has_side_effects=True),
        name="moe_dispatch",
    )(*args)


def _mlp_kernel(te_ref, nv_ref, hg_ref, wgu_ref, bgu_ref, wd_ref, bd_ref, o_ref):
    del te_ref
    nv = nv_ref[pl.program_id(0)]

    @pl.when(nv == 0)
    def _():
        o_ref[...] = jnp.zeros_like(o_ref)

    @pl.when(nv > 0)
    def _():
        f = wd_ref.shape[1]
        packed = hg_ref[...]
        row = lax.broadcasted_iota(jnp.int32, packed.shape, 0)
        packed = jnp.where(row < nv, packed, jnp.uint32(0))
        h = _unpack_pairs_f32(packed).astype(BF16)
        gu = jnp.dot(h, wgu_ref[0], preferred_element_type=F32) + bgu_ref[0]
        gate = jnp.minimum(gu[:, :f], SWIGLU_LIMIT)
        lin = jnp.clip(gu[:, f:], -SWIGLU_LIMIT, SWIGLU_LIMIT)
        act = gate * jax.nn.sigmoid(SWIGLU_ALPHA * gate) * (lin + 1.0)
        eo = jnp.dot(act.astype(BF16), wd_ref[0], preferred_element_type=F32) + bd_ref[0]
        o_ref[...] = _pack_pairs(eo)


def _expert_mlp(te, nv, hg, wgu_bf, bgu, wd_bf, bd, *, tm_e, n_tiles):
    n_exp, d, f2 = wgu_bf.shape
    f = wd_bf.shape[1]
    dp = hg.shape[1]
    grid_spec = pltpu.PrefetchScalarGridSpec(
        num_scalar_prefetch=2,
        grid=(n_tiles,),
        in_specs=[pl.BlockSpec((tm_e, dp), lambda i, te, nv: (i, 0)),
                  pl.BlockSpec((1, d, f2), lambda i, te, nv: (te[i], 0, 0)),
                  pl.BlockSpec((1, 1, f2), lambda i, te, nv: (te[i], 0, 0)),
                  pl.BlockSpec((1, f, d), lambda i, te, nv: (te[i], 0, 0)),
                  pl.BlockSpec((1, 1, d), lambda i, te, nv: (te[i], 0, 0))],
        out_specs=pl.BlockSpec((tm_e, dp), lambda i, te, nv: (i, 0)),
    )
    return pl.pallas_call(
        _mlp_kernel,
        out_shape=jax.ShapeDtypeStruct(hg.shape, jnp.uint32),
        grid_spec=grid_spec,
        compiler_params=_cparams("arbitrary"),
        name="expert_mlp",
    )(te, nv, hg, wgu_bf, bgu.reshape(n_exp, 1, f2), wd_bf, bd.reshape(n_exp, 1, d))


def _combine_kernel(dest_ref, dest_next_ref, tw_ref, x_ref, m_ref, eo_ref, o_ref, g_ref, sem):
    tt = x_ref.shape[0]
    groups = tt // SUBLANES
    dp = g_ref.shape[-1]
    i = pl.program_id(0)
    slot = i % 2

    def issue_all(idx_ref, sl):
        def issue(g, carry):
            for s in range(SUBLANES):
                for k in range(TOP_K):
                    d = idx_ref[0, 0, (g * SUBLANES + s) * TOP_K + k]
                    pltpu.make_async_copy(eo_ref.at[pl.ds(d, 1)], g_ref.at[sl, k, g, pl.ds(s, 1)],
                                          sem.at[sl]).start(priority=k % 2)
            return carry
        lax.fori_loop(0, groups, issue, 0)

    @pl.when(i == 0)
    def _():
        issue_all(dest_ref, 0)

    @pl.when(i + 1 < pl.num_programs(0))
    def _():
        issue_all(dest_next_ref, 1 - slot)

    pltpu.make_async_copy(g_ref.at[slot], g_ref.at[slot], sem.at[slot]).wait()
    y = None
    for k in range(TOP_K):
        part = tw_ref[:, k:k + 1] * _unpack_pairs_f32(g_ref[slot, k].reshape(tt, dp))
        y = part if y is None else y + part
    o_ref[...] = x_ref[...] + m_ref[0, 5:6, :] * y


def _combine(dest, tw_t, x2, mods, mod_row, eo, *, seq):
    n_tok, d = x2.shape
    dp = eo.shape[1]
    tt = _tile(seq, 512)
    tiles_per_seq = seq // tt
    dest_t = _dest_tiles(dest, tt)
    n_steps = n_tok // tt
    return pl.pallas_call(
        _combine_kernel,
        out_shape=jax.ShapeDtypeStruct((n_tok, d), F32),
        grid=(n_steps,),
        in_specs=[pl.BlockSpec((1, 1, tt * TOP_K), lambda i: (i, 0, 0), memory_space=pltpu.SMEM),
                  pl.BlockSpec((1, 1, tt * TOP_K), lambda i: (jnp.minimum(i + 1, n_steps - 1), 0, 0),
                               memory_space=pltpu.SMEM),
                  pl.BlockSpec((tt, TOP_K), lambda i: (i, 0)),
                  pl.BlockSpec((tt, d), lambda i: (i, 0)),
                  pl.BlockSpec((1, N_MOD, d), lambda i: (mod_row(i // tiles_per_seq), 0, 0)),
                  pl.BlockSpec(memory_space=pl.ANY)],
        out_specs=pl.BlockSpec((tt, d), lambda i: (i, 0)),
        scratch_shapes=[pltpu.VMEM((2, TOP_K, tt // SUBLANES, SUBLANES, dp), jnp.uint32),
                        pltpu.SemaphoreType.DMA((2,))],
        compiler_params=_cparams("arbitrary"),
        name="moe_combine",
    )(dest_t, dest_t, tw_t, x2, mods, eo)


def _rope_tables(seq, hd):
    n_freq = hd // 4
    pos = jnp.arange(seq)
    row = (pos // GRID_W).astype(F32)
    col = (pos % GRID_W).astype(F32)
    inv = ROPE_BASE ** (-jnp.arange(n_freq, dtype=F32) / n_freq)
    ang = jnp.concatenate([row[:, None] * inv, col[:, None] * inv], axis=1)
    ang = jnp.concatenate([ang, ang], axis=1)
    sign = jnp.where(jnp.arange(hd) < hd // 2, -1.0, 1.0)
    return jnp.cos(ang), sign * jnp.sin(ang)


def _half_major(a, hd):
    lead = a.shape[:-1]
    a = a.reshape(*lead, a.shape[-1] // hd, 2, 2, hd // 4)
    return jnp.swapaxes(a, -3, -2).reshape(*lead, -1)


def kernel(x, c, ctx, c_ctx, w_ada, b_ada, norm_mix_g, norm_ffn_g, w_in, q_norm_g, k_norm_g, lambda_qk,
           subln_g, conv_w, w_attn_out, w_conv_out, w_mix_out, w_router, b_router, w_gate_up, b_gate_up,
           w_down, b_down):
    batch, seq, d = x.shape
    n_ctx = ctx.shape[1]
    depth = w_ada.shape[0]
    hd = q_norm_g.shape[-1]
    n_cols = w_in.shape[-1]
    qkw = (n_cols - 5 * d) // 3
    n_heads = qkw // (2 * hd)
    n_exp = w_router.shape[-1]
    u_off = 3 * qkw
    ga_off = u_off + 3 * d

    cos, sin_s = _rope_tables(seq, hd)
    n_mod_rows = -(-(batch + 1) // 8) * 8
    cv = jnp.zeros((n_mod_rows, d), F32).at[:batch].set(c).at[batch].set(c_ctx)
    mods_all = _adaln(cv, w_ada, b_ada).reshape(depth, n_mod_rows, N_MOD, d)
    lat_row = lambda b: b
    ctx_row = lambda b: batch

    xs = x.reshape(batch * seq, d)
    cs = ctx.reshape(batch * n_ctx, d)
    for l in range(depth):
        last = l == depth - 1
        lam_init = 0.8 - 0.6 * math.exp(-0.3 * l)
        mods = mods_all[l]
        g_mix = norm_mix_g[l][None]
        g_ffn = norm_ffn_g[l][None]
        q_g = _half_major(q_norm_g[l][None], hd)
        k_g = _half_major(k_norm_g[l][None], hd)
        sub_g = subln_g[l][None]
        w_in_bf = _layer_bf16(w_in, l)
        w_in_bf = w_in_bf.at[:, :2 * qkw].set(_half_major(w_in_bf[:, :2 * qkw], hd))

        p_lat = _inproj(xs, mods, lat_row, g_mix, w_in_bf, q_g, k_g, cos, sin_s,
                        qkw=qkw, n_cols=n_cols, seq=seq, rope=True, hd=hd)
        p_ctx = _inproj(cs, mods, ctx_row, g_mix, w_in_bf, q_g, k_g, cos, sin_s,
                        qkw=qkw, n_cols=(3 * qkw if last else n_cols), seq=n_ctx, rope=False, hd=hd)
        attn_args = dict(batch=batch, n_heads=n_heads, hd=hd, qkw=qkw, lam_init=lam_init)
        y_attn, wgu_bf, wd_bf = _attention(lambda_qk[l], sub_g, p_lat, [p_ctx, p_lat], q_len=seq,
                                           kv_lens=[n_ctx, seq], ride=[(w_gate_up, l), (w_down, l)],
                                           **attn_args)
        y_conv = _short_conv(p_lat, conv_w[l], batch=batch, seq=seq, u_off=u_off, width=d)
        wa_bf = _layer_bf16(w_attn_out, l)
        wc_bf = _layer_bf16(w_conv_out, l)
        wm_bf = _layer_bf16(w_mix_out, l)
        w_router_t = w_router[l].T
        b_router_c = b_router[l][:, None]
        y_lat = _merge1(y_attn, y_conv, wa_bf, wc_bf, p_lat, ga_off=ga_off)
        xs, hp_lat, ids_lat, tw_lat = _merge2(y_lat, wm_bf, xs, mods, lat_row, g_ffn, w_router_t, b_router_c,
                                              seq=seq)
        if not last:
            y_attn_c, = _attention(lambda_qk[l], sub_g, p_ctx, [p_ctx], q_len=n_ctx, kv_lens=[n_ctx],
                                   **attn_args)
            y_conv_c = _short_conv(p_ctx, conv_w[l], batch=batch, seq=n_ctx, u_off=u_off, width=d)
            y_ctx = _merge1(y_attn_c, y_conv_c, wa_bf, wc_bf, p_ctx, ga_off=ga_off)
            cs, hp_ctx, ids_ctx, tw_ctx = _merge2(y_ctx, wm_bf, cs, mods, ctx_row, g_ffn, w_router_t,
                                                  b_router_c, seq=n_ctx)
            ids = jnp.concatenate([ids_ctx, ids_lat], axis=1)
        else:
            ids = ids_lat

        n_tok = ids.shape[1]
        n_pairs = n_tok * TOP_K
        tm_e = 512 if n_pairs >= 32768 else 32
        n_tiles = n_pairs // tm_e + n_exp
        n_tiles_pad = -(-n_tiles // LANES) * LANES
        dest, te, nv = _plan(ids, n_exp=n_exp, tm_e=tm_e, n_tiles_pad=n_tiles_pad)
        n_rows_pad = n_tiles * tm_e
        if not last:
            n_c = batch * n_ctx
            hg = _dispatch(dest[:, :n_c], hp_ctx, None, n_rows_pad=n_rows_pad)
            hg = _dispatch(dest[:, n_c:], hp_lat, hg, n_rows_pad=n_rows_pad)
        else:
            hg = _dispatch(dest, hp_lat, None, n_rows_pad=n_rows_pad)
        eo = _expert_mlp(te[0], nv[0], hg, wgu_bf, b_gate_up[l], wd_bf, b_down[l], tm_e=tm_e, n_tiles=n_tiles)
        if not last:
            cs = _combine(dest[:, :n_c], tw_ctx.T, cs, mods, ctx_row, eo, seq=n_ctx)
            xs = _combine(dest[:, n_c:], tw_lat.T, xs, mods, lat_row, eo, seq=seq)
        else:
            xs = _combine(dest, tw_lat.T, xs, mods, lat_row, eo, seq=seq)
    return xs.reshape(batch, seq, d)
```

```python
import functools
import math

import jax
import jax.numpy as jnp
from jax import lax
from jax.experimental import pallas as pl
from jax.experimental.pallas import tpu as pltpu

GRID_W = 64
TOP_K = 4
ROPE_BASE = 10000.0
RMS_EPS = 1e-6
SWIGLU_LIMIT = 7.0
SWIGLU_ALPHA = 1.702
N_MOD = 6
LANES = 128
SUBLANES = 8
CAST_BLOCK_ELEMS = 2 ** 21
INPROJ_ROW_BLOCK = 256
MERGE_ROW_BLOCK = 256
MIXOUT_ROW_BLOCK = 128
ATTN_ROW_BLOCK = 128
ATTN_KEY_CHUNK = 256
ATTN_SCORE_SLOTS = 8
VMEM_LIMIT_BYTES = 56 * 2 ** 20

F32 = jnp.float32
BF16 = jnp.bfloat16
NT_DIMS = (((1,), (1,)), ((), ()))


def _cparams(*sem):
    return pltpu.CompilerParams(dimension_semantics=sem, vmem_limit_bytes=VMEM_LIMIT_BYTES)


def _tile(n, pref):
    if n <= pref:
        return n
    for t in range(pref, 7, -1):
        if n % t == 0 and t % 8 == 0:
            return t
    return n


def _pack_pairs(x):
    n = x.shape[1] // 2
    hi = lax.bitcast_convert_type(x[:, :n].astype(BF16).astype(F32), jnp.uint32)
    lo = lax.bitcast_convert_type(x[:, n:].astype(BF16).astype(F32), jnp.uint32)
    return hi | (lo >> 16)


def _unpack_pairs_f32(p):
    left = lax.bitcast_convert_type(p & jnp.uint32(0xFFFF0000), F32)
    right = lax.bitcast_convert_type(p << 16, F32)
    return jnp.concatenate([left, right], axis=1)


def _rms(x):
    return x * lax.rsqrt(jnp.mean(x * x, axis=-1, keepdims=True) + RMS_EPS)


def _cast_kernel(w_ref, o_ref):
    o_ref[...] = w_ref[0].astype(BF16)


def _layer_bf16(w, layer):
    cols = w.shape[-1]
    w3 = w.reshape(w.shape[0], -1, cols)
    rows = w3.shape[1]
    tr = _tile(rows, max(8, CAST_BLOCK_ELEMS // cols))
    out = pl.pallas_call(
        _cast_kernel,
        out_shape=jax.ShapeDtypeStruct((rows, cols), BF16),
        grid=(rows // tr,),
        in_specs=[pl.BlockSpec((1, tr, cols), lambda i: (layer, i, 0))],
        out_specs=pl.BlockSpec((tr, cols), lambda i: (i, 0)),
        compiler_params=_cparams("arbitrary"),
        name="cast_bf16",
    )(w3)
    return out.reshape(w.shape[1:])


def _adaln_kernel(cv_ref, w_ref, b_ref, o_ref):
    cv = cv_ref[...]
    a = (cv * jax.nn.sigmoid(cv)).astype(BF16)
    o_ref[0] = jnp.dot(a, w_ref[0].astype(BF16), preferred_element_type=F32) + b_ref[0]


def _adaln(cv, w_ada, b_ada):
    n_layers, d, n = w_ada.shape
    r = cv.shape[0]
    tn = _tile(n, 1024)
    return pl.pallas_call(
        _adaln_kernel,
        out_shape=jax.ShapeDtypeStruct((n_layers, r, n), F32),
        grid=(n_layers, n // tn),
        in_specs=[pl.BlockSpec((r, d), lambda l, j: (0, 0)),
                  pl.BlockSpec((1, d, tn), lambda l, j: (l, 0, j)),
                  pl.BlockSpec((1, 1, tn), lambda l, j: (l, 0, j))],
        out_specs=pl.BlockSpec((1, r, tn), lambda l, j: (l, 0, j)),
        compiler_params=_cparams("arbitrary", "arbitrary"),
        name="adaln",
    )(cv, w_ada, b_ada.reshape(n_layers, 1, n))


def _inproj_kernel(x_ref, m_ref, g_ref, w_ref, qg_ref, kg_ref, cos_ref, sin_ref, o_ref, h_ref,
                   *, nq, rope, hd, q_scale):
    j = pl.program_id(1)

    tm, tn = o_ref.shape
    rb = min(tm, INPROJ_ROW_BLOCK)

    def qk_step(gain, first):
        if first:
            for r0 in range(0, tm, rb):
                h = _rms(x_ref[r0:r0 + rb, :]) * g_ref[...] * (1.0 + m_ref[0, 1:2, :]) + m_ref[0, 0:1, :]
                h_ref[r0:r0 + rb, :] = h.astype(BF16)
        accs = [jnp.dot(h_ref[r0:r0 + rb, :], w_ref[...], preferred_element_type=F32) for r0 in range(0, tm, rb)]
        for r0, acc in zip(range(0, tm, rb), accs):
            for g in range(0, tn, hd):
                n = _rms(acc[:, g:g + hd]) * gain
                if rope:
                    n = (n * cos_ref[r0:r0 + rb, :]
                         + pltpu.roll(n, hd // 2, axis=1) * sin_ref[r0:r0 + rb, :])
                o_ref[r0:r0 + rb, g:g + hd] = n.astype(BF16)

    @pl.when(j == 0)
    def _():
        qk_step(qg_ref[...] * q_scale, True)

    @pl.when((j > 0) & (j < nq))
    def _():
        qk_step(qg_ref[...] * q_scale, False)

    @pl.when((j >= nq) & (j < 2 * nq))
    def _():
        qk_step(kg_ref[...], False)

    @pl.when(j >= 2 * nq)
    def _():
        o_ref[...] = jnp.dot(h_ref[...], w_ref[...], preferred_element_type=F32).astype(BF16)


def _inproj(x2, mods, mod_row, norm_g, w_bf, q_g, k_g, cos, sin_s, *, qkw, n_cols, seq, rope, hd):
    rows, d = x2.shape
    tm = _tile(seq, 1024)
    tn = _tile(qkw, 1024)
    tiles_per_seq = seq // tm
    q_scale = float(hd) ** -0.5 * math.log2(math.e)
    kern = functools.partial(_inproj_kernel, nq=qkw // tn, rope=rope, hd=hd, q_scale=q_scale)
    return pl.pallas_call(
        kern,
        out_shape=jax.ShapeDtypeStruct((rows, n_cols), BF16),
        grid=(rows // tm, n_cols // tn),
        in_specs=[pl.BlockSpec((tm, d), lambda i, j: (i, 0)),
                  pl.BlockSpec((1, N_MOD, d), lambda i, j: (mod_row(i // tiles_per_seq), 0, 0)),
                  pl.BlockSpec((1, d), lambda i, j: (0, 0)),
                  pl.BlockSpec((d, tn), lambda i, j: (0, j)),
                  pl.BlockSpec((1, hd), lambda i, j: (0, 0)),
                  pl.BlockSpec((1, hd), lambda i, j: (0, 0)),
                  pl.BlockSpec((tm, hd), lambda i, j: ((i % tiles_per_seq) if rope else 0, 0)),
                  pl.BlockSpec((tm, hd), lambda i, j: ((i % tiles_per_seq) if rope else 0, 0))],
        out_specs=pl.BlockSpec((tm, tn), lambda i, j: (i, j)),
        scratch_shapes=[pltpu.VMEM((tm, d), BF16)],
        compiler_params=_cparams("arbitrary", "arbitrary"),
        name="inproj",
    )(x2, mods, norm_g, w_bf, q_g, k_g, cos, sin_s)


def _attn_kernel(lq_ref, sg_ref, q_ref, *refs, n_src, n_ride, lam_init, hd):
    k_refs = refs[0:2 * n_src:2]
    v_refs = refs[1:2 * n_src:2]
    ride_in = refs[2 * n_src:2 * n_src + n_ride]
    o_ref = refs[2 * n_src + n_ride]
    ride_out = refs[2 * n_src + n_ride + 1:2 * n_src + 2 * n_ride + 1]
    s_ref = refs[-1]
    for w_in_ref, w_out_ref in zip(ride_in, ride_out):
        w_out_ref[...] = w_in_ref[0].astype(BF16)
    lq = lq_ref[...]
    lam = (jnp.exp(jnp.sum(lq[0:1] * lq[1:2], axis=1, keepdims=True))
           - jnp.exp(jnp.sum(lq[2:3] * lq[3:4], axis=1, keepdims=True)) + lam_init)
    tq = q_ref.shape[0]
    rb = min(tq, ATTN_ROW_BLOCK)
    chunks = []
    col = 0
    for si in range(n_src):
        n = k_refs[si].shape[0]
        for st in range(0, n, ATTN_KEY_CHUNK):
            size = min(ATTN_KEY_CHUNK, n - st)
            chunks.append((si, st, size, col))
            col += size

    def lane_tiles(size):
        step = LANES if size % LANES == 0 else size
        return [(j, step) for j in range(0, size, step)]

    class RowReduce:
        def __init__(self, op, combine):
            self.op, self.combine, self.wide, self.narrow = op, combine, None, None

        def add(self, part):
            if part.shape[1] == LANES:
                self.wide = part if self.wide is None else self.combine(self.wide, part)
            else:
                part = self.op(part, axis=1, keepdims=True)
                self.narrow = part if self.narrow is None else self.combine(self.narrow, part)

        def result(self):
            out = [] if self.narrow is None else [self.narrow]
            if self.wide is not None:
                out.append(self.op(self.wide, axis=1, keepdims=True))
            return functools.reduce(self.combine, out)

    for r0 in range(0, tq, rb):
        maps = []
        for m in range(2):
            qm = q_ref[r0:r0 + rb, m * hd:(m + 1) * hd]
            slot = (2 * (r0 // rb) + m) % s_ref.shape[0]
            row_max = RowReduce(jnp.max, jnp.maximum)
            for si, st, size, col in chunks:
                s = lax.dot_general(qm, k_refs[si][st:st + size, m * hd:(m + 1) * hd], NT_DIMS,
                                    preferred_element_type=F32)
                s_ref[slot, :, col:col + size] = s
                for j, w in lane_tiles(size):
                    row_max.add(s[:, j:j + w])
            mx = row_max.result()
            mx_wide = jnp.broadcast_to(mx, (rb, LANES))
            row_sum = RowReduce(jnp.sum, jnp.add)
            pv = None
            for si, st, size, col in chunks:
                probs = []
                for j, w in lane_tiles(size):
                    e = jnp.exp2(s_ref[slot, :, col + j:col + j + w] - mx_wide[:, :w])
                    row_sum.add(e)
                    probs.append(e.astype(BF16))
                p = probs[0] if len(probs) == 1 else jnp.concatenate(probs, axis=1)
                d = jnp.dot(p, v_refs[si][st:st + size, :], preferred_element_type=F32)
                pv = d if pv is None else pv + d
            norm = row_sum.result()
            maps.append(pv * (1.0 / norm))
        o = maps[0] - lam * maps[1]
        o = _rms(o) * sg_ref[...] * (1.0 - lam_init)
        o_ref[r0:r0 + rb, :] = o.astype(BF16)


def _attention(lq, sub_g, q_arr, kv_arrs, *, batch, q_len, kv_lens, n_heads, hd, qkw, lam_init, ride=()):
    vd = 2 * hd
    tq = _tile(q_len, 2048)
    rb = min(tq, ATTN_ROW_BLOCK)
    nqb = q_len // tq
    k_off = qkw // vd
    v_off = 2 * qkw // vd
    in_specs = [pl.BlockSpec((4, hd), lambda b, h, i: (0, 0)),
                pl.BlockSpec((1, vd), lambda b, h, i: (0, 0)),
                pl.BlockSpec((tq, vd), lambda b, h, i: (b * nqb + i, h))]
    args = [lq, sub_g, q_arr]
    for arr, n in zip(kv_arrs, kv_lens):
        in_specs.append(pl.BlockSpec((n, vd), lambda b, h, i: (b, k_off + h)))
        in_specs.append(pl.BlockSpec((n, vd), lambda b, h, i: (b, v_off + h)))
        args += [arr, arr]
    n_steps = batch * n_heads * nqb
    out_shape = [jax.ShapeDtypeStruct((batch * q_len, n_heads * vd), BF16)]
    out_specs = [pl.BlockSpec((tq, vd), lambda b, h, i: (b * nqb + i, h))]
    for w, layer in ride:
        cols = w.shape[-1]
        w3 = w.reshape(w.shape[0], -1, cols)
        tr = w3.shape[1] // n_steps
        assert tr * n_steps == w3.shape[1] and tr % 8 == 0, (w.shape, n_steps)
        step = lambda b, h, i: (b * n_heads + h) * nqb + i
        in_specs.append(pl.BlockSpec((1, tr, cols), lambda b, h, i, layer=layer: (layer, step(b, h, i), 0)))
        args.append(w3)
        out_shape.append(jax.ShapeDtypeStruct((w3.shape[1], cols), BF16))
        out_specs.append(pl.BlockSpec((tr, cols), lambda b, h, i: (step(b, h, i), 0)))
    kern = functools.partial(_attn_kernel, n_src=len(kv_arrs), n_ride=len(ride), lam_init=lam_init, hd=hd)
    outs = pl.pallas_call(
        kern,
        out_shape=out_shape,
        grid=(batch, n_heads, nqb),
        in_specs=in_specs,
        out_specs=out_specs,
        scratch_shapes=[pltpu.VMEM((min(ATTN_SCORE_SLOTS, 2 * tq // rb), rb, sum(kv_lens)), F32)],
        compiler_params=_cparams("arbitrary", "arbitrary", "arbitrary"),
        name="diff_attn",
    )(*args)
    return [outs[0]] + [o.reshape(w.shape[1:]) for o, (w, _) in zip(outs[1:], ride)]


def _conv_kernel(u_ref, b_ref, c_ref, w_ref, o_ref):
    cu = c_ref[...].astype(F32) * u_ref[...].astype(F32)
    n = cu.shape[0]
    row = lax.broadcasted_iota(jnp.int32, cu.shape, 0)
    prev = jnp.where(row == 0, 0.0, pltpu.roll(cu, 1, axis=0))
    nxt = jnp.where(row == n - 1, 0.0, pltpu.roll(cu, n - 1, axis=0))
    w = w_ref[...]
    y = b_ref[...].astype(F32) * (w[0:1] * prev + w[1:2] * cu + w[2:3] * nxt)
    o_ref[...] = y.astype(BF16)


def _short_conv(p_arr, conv_w, *, batch, seq, u_off, width):
    tc = _tile(width, 512)
    o = u_off // tc
    nb = width // tc
    return pl.pallas_call(
        _conv_kernel,
        out_shape=jax.ShapeDtypeStruct((batch * seq, width), BF16),
        grid=(batch, nb),
        in_specs=[pl.BlockSpec((seq, tc), lambda b, j: (b, o + j)),
                  pl.BlockSpec((seq, tc), lambda b, j: (b, o + nb + j)),
                  pl.BlockSpec((seq, tc), lambda b, j: (b, o + 2 * nb + j)),
                  pl.BlockSpec((conv_w.shape[0], tc), lambda b, j: (0, j))],
        out_specs=pl.BlockSpec((seq, tc), lambda b, j: (b, j)),
        compiler_params=_cparams("arbitrary", "arbitrary"),
        name="short_conv",
    )(p_arr, p_arr, p_arr, conv_w)


def _merge1_kernel(ya_ref, yc_ref, wa_ref, wc_ref, ga_ref, gb_ref, o_ref):
    tm = o_ref.shape[0]
    rb = min(tm, MERGE_ROW_BLOCK)
    blocks = range(0, tm, rb)
    prods = [(jnp.dot(ya_ref[r0:r0 + rb, :], wa_ref[...], preferred_element_type=F32),
              jnp.dot(yc_ref[r0:r0 + rb, :], wc_ref[...], preferred_element_type=F32)) for r0 in blocks]
    for r0, (a, c) in zip(blocks, prods):
        rows = slice(r0, r0 + rb)
        y = (jax.nn.sigmoid(ga_ref[rows, :].astype(F32)) * a
             + jax.nn.sigmoid(gb_ref[rows, :].astype(F32)) * c)
        o_ref[rows, :] = y.astype(BF16)


def _merge1(y_attn, y_conv, wa_bf, wc_bf, p_arr, *, ga_off):
    rows, ka = y_attn.shape
    kc = y_conv.shape[1]
    d = wa_bf.shape[1]
    tm = _tile(rows, 1024)
    tn = _tile(d, 512)
    o = ga_off // tn
    nb = d // tn
    return pl.pallas_call(
        _merge1_kernel,
        out_shape=jax.ShapeDtypeStruct((rows, d), BF16),
        grid=(rows // tm, nb),
        in_specs=[pl.BlockSpec((tm, ka), lambda i, j: (i, 0)),
                  pl.BlockSpec((tm, kc), lambda i, j: (i, 0)),
                  pl.BlockSpec((ka, tn), lambda i, j: (0, j)),
                  pl.BlockSpec((kc, tn), lambda i, j: (0, j)),
                  pl.BlockSpec((tm, tn), lambda i, j: (i, o + j)),
                  pl.BlockSpec((tm, tn), lambda i, j: (i, o + nb + j))],
        out_specs=pl.BlockSpec((tm, tn), lambda i, j: (i, j)),
        compiler_params=_cparams("arbitrary", "arbitrary"),
        name="merge_branches",
    )(y_attn, y_conv, wa_bf, wc_bf, p_arr, p_arr)


def _split_bf16(x):
    hi = x.astype(BF16)
    return hi, (x - hi.astype(F32)).astype(BF16)


def _merge2_kernel(y_ref, wm_ref, x_ref, m_ref, gn_ref, wr_ref, br_ref, xo_ref, hp_ref, ids_ref, tw_ref):
    tm = x_ref.shape[0]
    rb = min(tm, MIXOUT_ROW_BLOCK)
    w_hi, w_lo = _split_bf16(wr_ref[...])
    mixes = [jnp.dot(y_ref[r0:r0 + rb, :], wm_ref[...], preferred_element_type=F32) for r0 in range(0, tm, rb)]
    for r0, mix in zip(range(0, tm, rb), mixes):
        rows = slice(r0, r0 + rb)
        xn = x_ref[rows, :] + m_ref[0, 2:3, :] * mix
        xo_ref[rows, :] = xn
        h = _rms(xn) * gn_ref[...] * (1.0 + m_ref[0, 4:5, :]) + m_ref[0, 3:4, :]
        hp_ref[rows, :] = _pack_pairs(h)

        h_hi, h_lo = _split_bf16(h)
        logits = (lax.dot_general(w_hi, h_hi, NT_DIMS, preferred_element_type=F32)
                  + lax.dot_general(w_hi, h_lo, NT_DIMS, preferred_element_type=F32)
                  + lax.dot_general(w_lo, h_hi, NT_DIMS, preferred_element_type=F32)) + br_ref[...]
        n_exp = logits.shape[0]
        expert = lax.broadcasted_iota(jnp.int32, logits.shape, 0)
        vals = []
        for k in range(TOP_K):
            mx = jnp.max(logits, axis=0, keepdims=True)
            idx = jnp.min(jnp.where(logits == mx, expert, n_exp), axis=0, keepdims=True)
            ids_ref[k:k + 1, rows] = idx
            vals.append(mx)
            logits = jnp.where(expert == idx, -jnp.inf, logits)
        es = [jnp.exp(v - vals[0]) for v in vals]
        inv = 1.0 / functools.reduce(jnp.add, es)
        for k in range(TOP_K):
            tw_ref[k:k + 1, rows] = es[k] * inv


def _merge2(y, wm_bf, x2, mods, mod_row, norm_g, w_router_t, b_router, *, seq):
    rows, d = x2.shape
    n_exp = w_router_t.shape[0]
    tm = _tile(seq, 512)
    tiles_per_seq = seq // tm
    return pl.pallas_call(
        _merge2_kernel,
        out_shape=(jax.ShapeDtypeStruct((rows, d), F32),
                   jax.ShapeDtypeStruct((rows, d // 2), jnp.uint32),
                   jax.ShapeDtypeStruct((TOP_K, rows), jnp.int32),
                   jax.ShapeDtypeStruct((TOP_K, rows), F32)),
        grid=(rows // tm,),
        in_specs=[pl.BlockSpec((tm, d), lambda i: (i, 0)),
                  pl.BlockSpec((d, d), lambda i: (0, 0)),
                  pl.BlockSpec((tm, d), lambda i: (i, 0)),
                  pl.BlockSpec((1, N_MOD, d), lambda i: (mod_row(i // tiles_per_seq), 0, 0)),
                  pl.BlockSpec((1, d), lambda i: (0, 0)),
                  pl.BlockSpec((n_exp, d), lambda i: (0, 0)),
                  pl.BlockSpec((n_exp, 1), lambda i: (0, 0))],
        out_specs=(pl.BlockSpec((tm, d), lambda i: (i, 0)),
                   pl.BlockSpec((tm, d // 2), lambda i: (i, 0)),
                   pl.BlockSpec((TOP_K, tm), lambda i: (0, i)),
                   pl.BlockSpec((TOP_K, tm), lambda i: (0, i))),
        compiler_params=_cparams("arbitrary"),
        name="mixout_router",
    )(y, wm_bf, x2, mods, norm_g, w_router_t, b_router)


def _plan_kernel(ids_ref, dest_ref, te_ref, nv_ref, cnt_ref, run_ref, gs_ref, *, n_exp, tm_e):
    p = pl.program_id(0)
    t = pl.program_id(1)
    ids = ids_ref[...]
    tt = ids.shape[1]
    expert = lax.broadcasted_iota(jnp.int32, (n_exp, tt), 0)
    hit = [ids[k:k + 1, :] == expert for k in range(TOP_K)]
    member = functools.reduce(jnp.add, [h.astype(F32) for h in hit])
    tile_count = jnp.sum(member, axis=1, keepdims=True)

    @pl.when((p == 0) & (t == 0))
    def _():
        cnt_ref[...] = jnp.zeros_like(cnt_ref)

    @pl.when(p == 0)
    def _():
        cnt_ref[...] += tile_count

    @pl.when((p == 1) & (t == 0))
    def _():
        cnt = cnt_ref[...]
        ptiles = jnp.floor((cnt + (tm_e - 1)) * (1.0 / tm_e))
        r = lax.broadcasted_iota(jnp.int32, (n_exp, n_exp), 0)
        cidx = lax.broadcasted_iota(jnp.int32, (n_exp, n_exp), 1)
        lower = (cidx < r).astype(BF16)
        p_hi = jnp.floor(ptiles * (1.0 / 16.0))
        p_lo = ptiles - 16.0 * p_hi
        gstart = (16.0 * jnp.dot(lower, p_hi.astype(BF16), preferred_element_type=F32)
                  + jnp.dot(lower, p_lo.astype(BF16), preferred_element_type=F32))
        gs_ref[...] = gstart
        run_ref[...] = jnp.zeros_like(run_ref)
        ntp = te_ref.shape[1]
        tile = lax.broadcasted_iota(jnp.int32, (n_exp, ntp), 1).astype(F32)
        e2 = lax.broadcasted_iota(jnp.int32, (n_exp, ntp), 0).astype(F32)
        gend = gstart[:, 0:1] + ptiles[:, 0:1]
        te = jnp.sum((gend <= tile).astype(F32), axis=0, keepdims=True)
        used = te < n_exp
        te = jnp.minimum(te, n_exp - 1.0)
        onehot = (e2 == te).astype(F32)
        start_e = jnp.sum(onehot * gstart[:, 0:1], axis=0, keepdims=True)
        cnt_e = jnp.sum(onehot * cnt[:, 0:1], axis=0, keepdims=True)
        nv = jnp.clip(cnt_e - (tile[0:1] - start_e) * tm_e, 0.0, float(tm_e))
        te_ref[...] = te.astype(jnp.int32)
        nv_ref[...] = jnp.where(used, nv, 0.0).astype(jnp.int32)

    @pl.when(p == 1)
    def _():
        rr = lax.broadcasted_iota(jnp.int32, (tt, tt), 0)
        cc = lax.broadcasted_iota(jnp.int32, (tt, tt), 1)
        upper = (rr < cc).astype(BF16)
        before = jnp.dot(member.astype(BF16), upper, preferred_element_type=F32)
        pos = gs_ref[:, 0:1] * tm_e + run_ref[:, 0:1] + before
        for k in range(TOP_K):
            d = jnp.sum(jnp.where(hit[k], pos, 0.0), axis=0, keepdims=True)
            dest_ref[k:k + 1, :] = d.astype(jnp.int32)
        run_ref[...] += tile_count


def _plan(ids, *, n_exp, tm_e, n_tiles_pad):
    n_tok = ids.shape[1]
    tt = _tile(n_tok, 512)
    kern = functools.partial(_plan_kernel, n_exp=n_exp, tm_e=tm_e)
    return pl.pallas_call(
        kern,
        out_shape=(jax.ShapeDtypeStruct((TOP_K, n_tok), jnp.int32),
                   jax.ShapeDtypeStruct((1, n_tiles_pad), jnp.int32),
                   jax.ShapeDtypeStruct((1, n_tiles_pad), jnp.int32)),
        grid=(2, n_tok // tt),
        in_specs=[pl.BlockSpec((TOP_K, tt), lambda p, t: (0, t))],
        out_specs=(pl.BlockSpec((TOP_K, tt), lambda p, t: (0, p * t)),
                   pl.BlockSpec((1, n_tiles_pad), lambda p, t: (0, 0)),
                   pl.BlockSpec((1, n_tiles_pad), lambda p, t: (0, 0))),
        scratch_shapes=[pltpu.VMEM((n_exp, LANES), F32)] * 3,
        compiler_params=_cparams("arbitrary", "arbitrary"),
        name="route_plan",
    )(ids)


def _dest_tiles(dest, tt):
    k, n = dest.shape
    return dest.T.reshape(n // tt, 1, tt * k)


def _dispatch_body(dest_ref, h_ref, hg_ref, sem):
    groups = h_ref.shape[0]

    def issue(g, carry):
        for s in range(SUBLANES):
            for k in range(TOP_K):
                d = dest_ref[0, 0, (g * SUBLANES + s) * TOP_K + k]
                pltpu.make_async_copy(h_ref.at[g, pl.ds(s, 1)], hg_ref.at[pl.ds(d, 1)],
                                      sem.at[k % 2]).start(priority=k % 2)
        return carry

    lax.fori_loop(0, groups, issue, 0)
    for p in range(2):
        for _ in range(TOP_K // 2):
            pltpu.make_async_copy(h_ref, h_ref, sem.at[p]).wait()


def _dispatch_first_kernel(dest_ref, h_ref, hg_ref, sem):
    _dispatch_body(dest_ref, h_ref, hg_ref, sem)


def _dispatch_next_kernel(dest_ref, h_ref, hg_in_ref, hg_ref, sem):
    del hg_in_ref
    _dispatch_body(dest_ref, h_ref, hg_ref, sem)


def _dispatch(dest, hp, hg, *, n_rows_pad):
    n_tok, dp = hp.shape
    tt = _tile(n_tok, 1024)
    dest_t = _dest_tiles(dest, tt)
    in_specs = [pl.BlockSpec((1, 1, tt * TOP_K), lambda i: (i, 0, 0), memory_space=pltpu.SMEM),
                pl.BlockSpec((tt // SUBLANES, SUBLANES, dp), lambda i: (i, 0, 0))]
    args = [dest_t, hp.reshape(n_tok // SUBLANES, SUBLANES, dp)]
    aliases = {}
    kern = _dispatch_first_kernel
    if hg is not None:
        in_specs.append(pl.BlockSpec(memory_space=pl.ANY))
        args.append(hg)
        aliases = {2: 0}
        kern = _dispatch_next_kernel
    return pl.pallas_call(
        kern,
        out_shape=jax.ShapeDtypeStruct((n_rows_pad, dp), jnp.uint32),
        grid=(n_tok // tt,),
        in_specs=in_specs,
        out_specs=pl.BlockSpec(memory_space=pl.ANY),
        scratch_shapes=[pltpu.SemaphoreType.DMA((2,))],
        input_output_aliases=aliases,
        compiler_params=pltpu.CompilerParams(dimension_semantics=("arbitrary",),
                                             vmem_limit_bytes=VMEM_LIMIT_BYTES, has_side_effects=True),
        name="moe_dispatch",
    )(*args)


def _mlp_kernel(te_ref, nv_ref, hg_ref, wgu_ref, bgu_ref, wd_ref, bd_ref, o_ref):
    del te_ref
    nv = nv_ref[pl.program_id(0)]

    @pl.when(nv == 0)
    def _():
        o_ref[...] = jnp.zeros_like(o_ref)

    @pl.when(nv > 0)
    def _():
        f = wd_ref.shape[1]
        packed = hg_ref[...]
        row = lax.broadcasted_iota(jnp.int32, packed.shape, 0)
        packed = jnp.where(row < nv, packed, jnp.uint32(0))
        h = _unpack_pairs_f32(packed).astype(BF16)
        gu = jnp.dot(h, wgu_ref[0], preferred_element_type=F32) + bgu_ref[0]
        gate = jnp.minimum(gu[:, :f], SWIGLU_LIMIT)
        lin = jnp.clip(gu[:, f:], -SWIGLU_LIMIT, SWIGLU_LIMIT)
        act = gate * jax.nn.sigmoid(SWIGLU_ALPHA * gate) * (lin + 1.0)
        eo = jnp.dot(act.astype(BF16), wd_ref[0], preferred_element_type=F32) + bd_ref[0]
        o_ref[...] = _pack_pairs(eo)


def _expert_mlp(te, nv, hg, wgu_bf, bgu, wd_bf, bd, *, tm_e, n_tiles):
    n_exp, d, f2 = wgu_bf.shape
    f = wd_bf.shape[1]
    dp = hg.shape[1]
    grid_spec = pltpu.PrefetchScalarGridSpec(
        num_scalar_prefetch=2,
        grid=(n_tiles,),
        in_specs=[pl.BlockSpec((tm_e, dp), lambda i, te, nv: (i, 0)),
                  pl.BlockSpec((1, d, f2), lambda i, te, nv: (te[i], 0, 0)),
                  pl.BlockSpec((1, 1, f2), lambda i, te, nv: (te[i], 0, 0)),
                  pl.BlockSpec((1, f, d), lambda i, te, nv: (te[i], 0, 0)),
                  pl.BlockSpec((1, 1, d), lambda i, te, nv: (te[i], 0, 0))],
        out_specs=pl.BlockSpec((tm_e, dp), lambda i, te, nv: (i, 0)),
    )
    return pl.pallas_call(
        _mlp_kernel,
        out_shape=jax.ShapeDtypeStruct(hg.shape, jnp.uint32),
        grid_spec=grid_spec,
        compiler_params=_cparams("arbitrary"),
        name="expert_mlp",
    )(te, nv, hg, wgu_bf, bgu.reshape(n_exp, 1, f2), wd_bf, bd.reshape(n_exp, 1, d))


def _combine_kernel(dest_ref, dest_next_ref, tw_ref, x_ref, m_ref, eo_ref, o_ref, g_ref, sem):
    tt = x_ref.shape[0]
    groups = tt // SUBLANES
    dp = g_ref.shape[-1]
    i = pl.program_id(0)
    slot = i % 2

    def issue_all(idx_ref, sl):
        def issue(g, carry):
            for s in range(SUBLANES):
                for k in range(TOP_K):
                    d = idx_ref[0, 0, (g * SUBLANES + s) * TOP_K + k]
                    pltpu.make_async_copy(eo_ref.at[pl.ds(d, 1)], g_ref.at[sl, k, g, pl.ds(s, 1)],
                                          sem.at[sl]).start(priority=k % 2)
            return carry
        lax.fori_loop(0, groups, issue, 0)

    @pl.when(i == 0)
    def _():
        issue_all(dest_ref, 0)

    @pl.when(i + 1 < pl.num_programs(0))
    def _():
        issue_all(dest_next_ref, 1 - slot)

    pltpu.make_async_copy(g_ref.at[slot], g_ref.at[slot], sem.at[slot]).wait()
    y = None
    for k in range(TOP_K):
        part = tw_ref[:, k:k + 1] * _unpack_pairs_f32(g_ref[slot, k].reshape(tt, dp))
        y = part if y is None else y + part
    o_ref[...] = x_ref[...] + m_ref[0, 5:6, :] * y


def _combine(dest, tw_t, x2, mods, mod_row, eo, *, seq):
    n_tok, d = x2.shape
    dp = eo.shape[1]
    tt = _tile(seq, 256)
    tiles_per_seq = seq // tt
    dest_t = _dest_tiles(dest, tt)
    n_steps = n_tok // tt
    return pl.pallas_call(
        _combine_kernel,
        out_shape=jax.ShapeDtypeStruct((n_tok, d), F32),
        grid=(n_steps,),
        in_specs=[pl.BlockSpec((1, 1, tt * TOP_K), lambda i: (i, 0, 0), memory_space=pltpu.SMEM),
                  pl.BlockSpec((1, 1, tt * TOP_K), lambda i: (jnp.minimum(i + 1, n_steps - 1), 0, 0),
                               memory_space=pltpu.SMEM),
                  pl.BlockSpec((tt, TOP_K), lambda i: (i, 0)),
                  pl.BlockSpec((tt, d), lambda i: (i, 0)),
                  pl.BlockSpec((1, N_MOD, d), lambda i: (mod_row(i // tiles_per_seq), 0, 0)),
                  pl.BlockSpec(memory_space=pl.ANY)],
        out_specs=pl.BlockSpec((tt, d), lambda i: (i, 0)),
        scratch_shapes=[pltpu.VMEM((2, TOP_K, tt // SUBLANES, SUBLANES, dp), jnp.uint32),
                        pltpu.SemaphoreType.DMA((2,))],
        compiler_params=_cparams("arbitrary"),
        name="moe_combine",
    )(dest_t, dest_t, tw_t, x2, mods, eo)


def _rope_tables(seq, hd):
    n_freq = hd // 4
    pos = jnp.arange(seq)
    row = (pos // GRID_W).astype(F32)
    col = (pos % GRID_W).astype(F32)
    inv = ROPE_BASE ** (-jnp.arange(n_freq, dtype=F32) / n_freq)
    ang = jnp.concatenate([row[:, None] * inv, col[:, None] * inv], axis=1)
    ang = jnp.concatenate([ang, ang], axis=1)
    sign = jnp.where(jnp.arange(hd) < hd // 2, -1.0, 1.0)
    return jnp.cos(ang), sign * jnp.sin(ang)


def _half_major(a, hd):
    lead = a.shape[:-1]
    a = a.reshape(*lead, a.shape[-1] // hd, 2, 2, hd // 4)
    return jnp.swapaxes(a, -3, -2).reshape(*lead, -1)


def kernel(x, c, ctx, c_ctx, w_ada, b_ada, norm_mix_g, norm_ffn_g, w_in, q_norm_g, k_norm_g, lambda_qk,
           subln_g, conv_w, w_attn_out, w_conv_out, w_mix_out, w_router, b_router, w_gate_up, b_gate_up,
           w_down, b_down):
    batch, seq, d = x.shape
    n_ctx = ctx.shape[1]
    depth = w_ada.shape[0]
    hd = q_norm_g.shape[-1]
    n_cols = w_in.shape[-1]
    qkw = (n_cols - 5 * d) // 3
    n_heads = qkw // (2 * hd)
    n_exp = w_router.shape[-1]
    u_off = 3 * qkw
    ga_off = u_off + 3 * d

    cos, sin_s = _rope_tables(seq, hd)
    n_mod_rows = -(-(batch + 1) // 8) * 8
    cv = jnp.zeros((n_mod_rows, d), F32).at[:batch].set(c).at[batch].set(c_ctx)
    mods_all = _adaln(cv, w_ada, b_ada).reshape(depth, n_mod_rows, N_MOD, d)
    lat_row = lambda b: b
    ctx_row = lambda b: batch

    xs = x.reshape(batch * seq, d)
    cs = ctx.reshape(batch * n_ctx, d)
    for l in range(depth):
        last = l == depth - 1
        lam_init = 0.8 - 0.6 * math.exp(-0.3 * l)
        mods = mods_all[l]
        g_mix = norm_mix_g[l][None]
        g_ffn = norm_ffn_g[l][None]
        q_g = _half_major(q_norm_g[l][None], hd)
        k_g = _half_major(k_norm_g[l][None], hd)
        sub_g = subln_g[l][None]
        w_in_bf = _layer_bf16(w_in, l) if l == 0 else w_in_next
        w_in_bf = w_in_bf.at[:, :2 * qkw].set(_half_major(w_in_bf[:, :2 * qkw], hd))

        p_lat = _inproj(xs, mods, lat_row, g_mix, w_in_bf, q_g, k_g, cos, sin_s,
                        qkw=qkw, n_cols=n_cols, seq=seq, rope=True, hd=hd)
        p_ctx = _inproj(cs, mods, ctx_row, g_mix, w_in_bf, q_g, k_g, cos, sin_s,
                        qkw=qkw, n_cols=(3 * qkw if last else n_cols), seq=n_ctx, rope=False, hd=hd)
        attn_args = dict(batch=batch, n_heads=n_heads, hd=hd, qkw=qkw, lam_init=lam_init)
        ride = [(w_gate_up, l), (w_down, l), (w_attn_out, l), (w_conv_out, l), (w_mix_out, l)]
        if not last:
            ride.append((w_in, l + 1))
        y_attn, wgu_bf, wd_bf, wa_bf, wc_bf, wm_bf, *rest = _attention(
            lambda_qk[l], sub_g, p_lat, [p_ctx, p_lat], q_len=seq, kv_lens=[n_ctx, seq], ride=ride, **attn_args)
        if not last:
            w_in_next, = rest
        y_conv = _short_conv(p_lat, conv_w[l], batch=batch, seq=seq, u_off=u_off, width=d)
        w_router_t = w_router[l].T
        b_router_c = b_router[l][:, None]
        y_lat = _merge1(y_attn, y_conv, wa_bf, wc_bf, p_lat, ga_off=ga_off)
        xs, hp_lat, ids_lat, tw_lat = _merge2(y_lat, wm_bf, xs, mods, lat_row, g_ffn, w_router_t, b_router_c,
                                              seq=seq)
        if not last:
            y_attn_c, = _attention(lambda_qk[l], sub_g, p_ctx, [p_ctx], q_len=n_ctx, kv_lens=[n_ctx],
                                   **attn_args)
            y_conv_c = _short_conv(p_ctx, conv_w[l], batch=batch, seq=n_ctx, u_off=u_off, width=d)
            y_ctx = _merge1(y_attn_c, y_conv_c, wa_bf, wc_bf, p_ctx, ga_off=ga_off)
            cs, hp_ctx, ids_ctx, tw_ctx = _merge2(y_ctx, wm_bf, cs, mods, ctx_row, g_ffn, w_router_t,
                                                  b_router_c, seq=n_ctx)
            ids = jnp.concatenate([ids_ctx, ids_lat], axis=1)
        else:
            ids = ids_lat

        n_tok = ids.shape[1]
        n_pairs = n_tok * TOP_K
        tm_e = 512 if n_pairs >= 32768 else 32
        n_tiles = n_pairs // tm_e + n_exp
        n_tiles_pad = -(-n_tiles // LANES) * LANES
        dest, te, nv = _plan(ids, n_exp=n_exp, tm_e=tm_e, n_tiles_pad=n_tiles_pad)
        n_rows_pad = n_tiles * tm_e
        if not last:
            n_c = batch * n_ctx
            hg = _dispatch(dest[:, :n_c], hp_ctx, None, n_rows_pad=n_rows_pad)
            hg = _dispatch(dest[:, n_c:], hp_lat, hg, n_rows_pad=n_rows_pad)
        else:
            hg = _dispatch(dest, hp_lat, None, n_rows_pad=n_rows_pad)
        eo = _expert_mlp(te[0], nv[0], hg, wgu_bf, b_gate_up[l], wd_bf, b_down[l], tm_e=tm_e, n_tiles=n_tiles)
        if not last:
            cs = _combine(dest[:, :n_c], tw_ctx.T, cs, mods, ctx_row, eo, seq=n_ctx)
            xs = _combine(dest[:, n_c:], tw_lat.T, xs, mods, lat_row, eo, seq=seq)
        else:
            xs = _combine(dest, tw_lat.T, xs, mods, lat_row, eo, seq=seq)
    return xs.reshape(batch, seq, d)
```

```python
import functools
import math

import jax
import jax.numpy as jnp
from jax import lax
from jax.experimental import pallas as pl
from jax.experimental.pallas import tpu as pltpu

GRID_W = 64
TOP_K = 4
ROPE_BASE = 10000.0
RMS_EPS = 1e-6
SWIGLU_LIMIT = 7.0
SWIGLU_ALPHA = 1.702
N_MOD = 6
LANES = 128
SUBLANES = 8
CAST_BLOCK_ELEMS = 2 ** 21
INPROJ_ROW_BLOCK = 256
MERGE_ROW_BLOCK = 256
MIXOUT_ROW_BLOCK = 128
ATTN_ROW_BLOCK = 128
ATTN_KEY_CHUNK = 256
ATTN_SCORE_SLOTS = 8
VMEM_LIMIT_BYTES = 56 * 2 ** 20

F32 = jnp.float32
BF16 = jnp.bfloat16
NT_DIMS = (((1,), (1,)), ((), ()))


def _cparams(*sem):
    return pltpu.CompilerParams(dimension_semantics=sem, vmem_limit_bytes=VMEM_LIMIT_BYTES)


def _tile(n, pref):
    if n <= pref:
        return n
    for t in range(pref, 7, -1):
        if n % t == 0 and t % 8 == 0:
            return t
    return n


def _pack_pairs(x):
    n = x.shape[1] // 2
    hi = lax.bitcast_convert_type(x[:, :n].astype(BF16).astype(F32), jnp.uint32)
    lo = lax.bitcast_convert_type(x[:, n:].astype(BF16).astype(F32), jnp.uint32)
    return hi | (lo >> 16)


def _unpack_pairs_f32(p):
    left = lax.bitcast_convert_type(p & jnp.uint32(0xFFFF0000), F32)
    right = lax.bitcast_convert_type(p << 16, F32)
    return jnp.concatenate([left, right], axis=1)


def _rms(x):
    return x * lax.rsqrt(jnp.mean(x * x, axis=-1, keepdims=True) + RMS_EPS)


def _cast_kernel(w_ref, o_ref):
    o_ref[...] = w_ref[0].astype(BF16)


def _layer_bf16(w, layer):
    cols = w.shape[-1]
    w3 = w.reshape(w.shape[0], -1, cols)
    rows = w3.shape[1]
    tr = _tile(rows, max(8, CAST_BLOCK_ELEMS // cols))
    out = pl.pallas_call(
        _cast_kernel,
        out_shape=jax.ShapeDtypeStruct((rows, cols), BF16),
        grid=(rows // tr,),
        in_specs=[pl.BlockSpec((1, tr, cols), lambda i: (layer, i, 0))],
        out_specs=pl.BlockSpec((tr, cols), lambda i: (i, 0)),
        compiler_params=_cparams("arbitrary"),
        name="cast_bf16",
    )(w3)
    return out.reshape(w.shape[1:])


def _adaln_kernel(cv_ref, w_ref, b_ref, o_ref):
    cv = cv_ref[...]
    a = (cv * jax.nn.sigmoid(cv)).astype(BF16)
    o_ref[0] = jnp.dot(a, w_ref[0].astype(BF16), preferred_element_type=F32) + b_ref[0]


def _adaln(cv, w_ada, b_ada):
    n_layers, d, n = w_ada.shape
    r = cv.shape[0]
    tn = _tile(n, 1024)
    return pl.pallas_call(
        _adaln_kernel,
        out_shape=jax.ShapeDtypeStruct((n_layers, r, n), F32),
        grid=(n_layers, n // tn),
        in_specs=[pl.BlockSpec((r, d), lambda l, j: (0, 0)),
                  pl.BlockSpec((1, d, tn), lambda l, j: (l, 0, j)),
                  pl.BlockSpec((1, 1, tn), lambda l, j: (l, 0, j))],
        out_specs=pl.BlockSpec((1, r, tn), lambda l, j: (l, 0, j)),
        compiler_params=_cparams("arbitrary", "arbitrary"),
        name="adaln",
    )(cv, w_ada, b_ada.reshape(n_layers, 1, n))


def _inproj_kernel(x_ref, m_ref, g_ref, w_ref, qg_ref, kg_ref, cos_ref, sin_ref, o_ref, h_ref,
                   *, nq, rope, hd, q_scale):
    j = pl.program_id(1)

    tm, tn = o_ref.shape
    rb = min(tm, INPROJ_ROW_BLOCK)

    def qk_step(gain, first):
        if first:
            for r0 in range(0, tm, rb):
                h = _rms(x_ref[r0:r0 + rb, :]) * g_ref[...] * (1.0 + m_ref[0, 1:2, :]) + m_ref[0, 0:1, :]
                h_ref[r0:r0 + rb, :] = h.astype(BF16)
        accs = [jnp.dot(h_ref[r0:r0 + rb, :], w_ref[...], preferred_element_type=F32) for r0 in range(0, tm, rb)]
        for r0, acc in zip(range(0, tm, rb), accs):
            for g in range(0, tn, hd):
                n = _rms(acc[:, g:g + hd]) * gain
                if rope:
                    n = (n * cos_ref[r0:r0 + rb, :]
                         + pltpu.roll(n, hd // 2, axis=1) * sin_ref[r0:r0 + rb, :])
                o_ref[r0:r0 + rb, g:g + hd] = n.astype(BF16)

    @pl.when(j == 0)
    def _():
        qk_step(qg_ref[...] * q_scale, True)

    @pl.when((j > 0) & (j < nq))
    def _():
        qk_step(qg_ref[...] * q_scale, False)

    @pl.when((j >= nq) & (j < 2 * nq))
    def _():
        qk_step(kg_ref[...], False)

    @pl.when(j >= 2 * nq)
    def _():
        o_ref[...] = jnp.dot(h_ref[...], w_ref[...], preferred_element_type=F32).astype(BF16)


def _inproj(x2, mods, mod_row, norm_g, w_bf, q_g, k_g, cos, sin_s, *, qkw, n_cols, seq, rope, hd):
    rows, d = x2.shape
    tm = _tile(seq, 1024)
    tn = _tile(qkw, 1024)
    tiles_per_seq = seq // tm
    q_scale = float(hd) ** -0.5 * math.log2(math.e)
    kern = functools.partial(_inproj_kernel, nq=qkw // tn, rope=rope, hd=hd, q_scale=q_scale)
    return pl.pallas_call(
        kern,
        out_shape=jax.ShapeDtypeStruct((rows, n_cols), BF16),
        grid=(rows // tm, n_cols // tn),
        in_specs=[pl.BlockSpec((tm, d), lambda i, j: (i, 0)),
                  pl.BlockSpec((1, N_MOD, d), lambda i, j: (mod_row(i // tiles_per_seq), 0, 0)),
                  pl.BlockSpec((1, d), lambda i, j: (0, 0)),
                  pl.BlockSpec((d, tn), lambda i, j: (0, j)),
                  pl.BlockSpec((1, hd), lambda i, j: (0, 0)),
                  pl.BlockSpec((1, hd), lambda i, j: (0, 0)),
                  pl.BlockSpec((tm, hd), lambda i, j: ((i % tiles_per_seq) if rope else 0, 0)),
                  pl.BlockSpec((tm, hd), lambda i, j: ((i % tiles_per_seq) if rope else 0, 0))],
        out_specs=pl.BlockSpec((tm, tn), lambda i, j: (i, j)),
        scratch_shapes=[pltpu.VMEM((tm, d), BF16)],
        compiler_params=_cparams("arbitrary", "arbitrary"),
        name="inproj",
    )(x2, mods, norm_g, w_bf, q_g, k_g, cos, sin_s)


def _attn_kernel(lq_ref, sg_ref, q_ref, *refs, n_src, n_ride, lam_init, hd):
    k_refs = refs[0:2 * n_src:2]
    v_refs = refs[1:2 * n_src:2]
    ride_in = refs[2 * n_src:2 * n_src + n_ride]
    o_ref = refs[2 * n_src + n_ride]
    ride_out = refs[2 * n_src + n_ride + 1:2 * n_src + 2 * n_ride + 1]
    s_ref = refs[-1]
    for w_in_ref, w_out_ref in zip(ride_in, ride_out):
        w_out_ref[...] = w_in_ref[0].astype(BF16)
    lq = lq_ref[...]
    lam = (jnp.exp(jnp.sum(lq[0:1] * lq[1:2], axis=1, keepdims=True))
           - jnp.exp(jnp.sum(lq[2:3] * lq[3:4], axis=1, keepdims=True)) + lam_init)
    tq = q_ref.shape[0]
    rb = min(tq, ATTN_ROW_BLOCK)
    chunks = []
    col = 0
    for si in range(n_src):
        n = k_refs[si].shape[0]
        for st in range(0, n, ATTN_KEY_CHUNK):
            size = min(ATTN_KEY_CHUNK, n - st)
            chunks.append((si, st, size, col))
            col += size

    def lane_tiles(size):
        step = LANES if size % LANES == 0 else size
        return [(j, step) for j in range(0, size, step)]

    class RowReduce:
        def __init__(self, op, combine):
            self.op, self.combine, self.wide, self.narrow = op, combine, None, None

        def add(self, part):
            if part.shape[1] == LANES:
                self.wide = part if self.wide is None else self.combine(self.wide, part)
            else:
                part = self.op(part, axis=1, keepdims=True)
                self.narrow = part if self.narrow is None else self.combine(self.narrow, part)

        def result(self):
            out = [] if self.narrow is None else [self.narrow]
            if self.wide is not None:
                out.append(self.op(self.wide, axis=1, keepdims=True))
            return functools.reduce(self.combine, out)

    for r0 in range(0, tq, rb):
        maps = []
        for m in range(2):
            qm = q_ref[r0:r0 + rb, m * hd:(m + 1) * hd]
            slot = (2 * (r0 // rb) + m) % s_ref.shape[0]
            row_max = RowReduce(jnp.max, jnp.maximum)
            for si, st, size, col in chunks:
                s = lax.dot_general(qm, k_refs[si][st:st + size, m * hd:(m + 1) * hd], NT_DIMS,
                                    preferred_element_type=F32)
                s_ref[slot, :, col:col + size] = s
                for j, w in lane_tiles(size):
                    row_max.add(s[:, j:j + w])
            mx = row_max.result()
            mx_wide = jnp.broadcast_to(mx, (rb, LANES))
            row_sum = RowReduce(jnp.sum, jnp.add)
            pv = None
            for si, st, size, col in chunks:
                probs = []
                for j, w in lane_tiles(size):
                    e = jnp.exp2(s_ref[slot, :, col + j:col + j + w] - mx_wide[:, :w])
                    row_sum.add(e)
                    probs.append(e.astype(BF16))
                p = probs[0] if len(probs) == 1 else jnp.concatenate(probs, axis=1)
                d = jnp.dot(p, v_refs[si][st:st + size, :], preferred_element_type=F32)
                pv = d if pv is None else pv + d
            norm = row_sum.result()
            maps.append(pv * (1.0 / norm))
        o = maps[0] - lam * maps[1]
        o = _rms(o) * sg_ref[...] * (1.0 - lam_init)
        o_ref[r0:r0 + rb, :] = o.astype(BF16)


def _attention(lq, sub_g, q_arr, kv_arrs, *, batch, q_len, kv_lens, n_heads, hd, qkw, lam_init, ride=()):
    vd = 2 * hd
    tq = _tile(q_len, 2048)
    rb = min(tq, ATTN_ROW_BLOCK)
    nqb = q_len // tq
    k_off = qkw // vd
    v_off = 2 * qkw // vd
    in_specs = [pl.BlockSpec((4, hd), lambda b, h, i: (0, 0)),
                pl.BlockSpec((1, vd), lambda b, h, i: (0, 0)),
                pl.BlockSpec((tq, vd), lambda b, h, i: (b * nqb + i, h))]
    args = [lq, sub_g, q_arr]
    for arr, n in zip(kv_arrs, kv_lens):
        in_specs.append(pl.BlockSpec((n, vd), lambda b, h, i: (b, k_off + h)))
        in_specs.append(pl.BlockSpec((n, vd), lambda b, h, i: (b, v_off + h)))
        args += [arr, arr]
    n_steps = batch * n_heads * nqb
    out_shape = [jax.ShapeDtypeStruct((batch * q_len, n_heads * vd), BF16)]
    out_specs = [pl.BlockSpec((tq, vd), lambda b, h, i: (b * nqb + i, h))]
    for w, layer in ride:
        cols = w.shape[-1]
        w3 = w.reshape(w.shape[0], -1, cols)
        tr = w3.shape[1] // n_steps
        assert tr * n_steps == w3.shape[1] and tr % 8 == 0, (w.shape, n_steps)
        step = lambda b, h, i: (b * n_heads + h) * nqb + i
        in_specs.append(pl.BlockSpec((1, tr, cols), lambda b, h, i, layer=layer: (layer, step(b, h, i), 0)))
        args.append(w3)
        out_shape.append(jax.ShapeDtypeStruct((w3.shape[1], cols), BF16))
        out_specs.append(pl.BlockSpec((tr, cols), lambda b, h, i: (step(b, h, i), 0)))
    kern = functools.partial(_attn_kernel, n_src=len(kv_arrs), n_ride=len(ride), lam_init=lam_init, hd=hd)
    outs = pl.pallas_call(
        kern,
        out_shape=out_shape,
        grid=(batch, n_heads, nqb),
        in_specs=in_specs,
        out_specs=out_specs,
        scratch_shapes=[pltpu.VMEM((min(ATTN_SCORE_SLOTS, 2 * tq // rb), rb, sum(kv_lens)), F32)],
        compiler_params=_cparams("arbitrary", "arbitrary", "arbitrary"),
        name="diff_attn",
    )(*args)
    return [outs[0]] + [o.reshape(w.shape[1:]) for o, (w, _) in zip(outs[1:], ride)]


def _conv_kernel(u_ref, b_ref, c_ref, w_ref, o_ref):
    cu = c_ref[...].astype(F32) * u_ref[...].astype(F32)
    n = cu.shape[0]
    row = lax.broadcasted_iota(jnp.int32, cu.shape, 0)
    prev = jnp.where(row == 0, 0.0, pltpu.roll(cu, 1, axis=0))
    nxt = jnp.where(row == n - 1, 0.0, pltpu.roll(cu, n - 1, axis=0))
    w = w_ref[...]
    y = b_ref[...].astype(F32) * (w[0:1] * prev + w[1:2] * cu + w[2:3] * nxt)
    o_ref[...] = y.astype(BF16)


def _short_conv(p_arr, conv_w, *, batch, seq, u_off, width):
    tc = _tile(width, 512)
    o = u_off // tc
    nb = width // tc
    return pl.pallas_call(
        _conv_kernel,
        out_shape=jax.ShapeDtypeStruct((batch * seq, width), BF16),
        grid=(batch, nb),
        in_specs=[pl.BlockSpec((seq, tc), lambda b, j: (b, o + j)),
                  pl.BlockSpec((seq, tc), lambda b, j: (b, o + nb + j)),
                  pl.BlockSpec((seq, tc), lambda b, j: (b, o + 2 * nb + j)),
                  pl.BlockSpec((conv_w.shape[0], tc), lambda b, j: (0, j))],
        out_specs=pl.BlockSpec((seq, tc), lambda b, j: (b, j)),
        compiler_params=_cparams("arbitrary", "arbitrary"),
        name="short_conv",
    )(p_arr, p_arr, p_arr, conv_w)


def _merge1_kernel(ya_ref, yc_ref, wa_ref, wc_ref, ga_ref, gb_ref, o_ref):
    tm = o_ref.shape[0]
    rb = min(tm, MERGE_ROW_BLOCK)
    blocks = range(0, tm, rb)
    prods = [(jnp.dot(ya_ref[r0:r0 + rb, :], wa_ref[...], preferred_element_type=F32),
              jnp.dot(yc_ref[r0:r0 + rb, :], wc_ref[...], preferred_element_type=F32)) for r0 in blocks]
    for r0, (a, c) in zip(blocks, prods):
        rows = slice(r0, r0 + rb)
        y = (jax.nn.sigmoid(ga_ref[rows, :].astype(F32)) * a
             + jax.nn.sigmoid(gb_ref[rows, :].astype(F32)) * c)
        o_ref[rows, :] = y.astype(BF16)


def _merge1(y_attn, y_conv, wa_bf, wc_bf, p_arr, *, ga_off):
    rows, ka = y_attn.shape
    kc = y_conv.shape[1]
    d = wa_bf.shape[1]
    tm = _tile(rows, 1024)
    tn = _tile(d, 512)
    o = ga_off // tn
    nb = d // tn
    return pl.pallas_call(
        _merge1_kernel,
        out_shape=jax.ShapeDtypeStruct((rows, d), BF16),
        grid=(rows // tm, nb),
        in_specs=[pl.BlockSpec((tm, ka), lambda i, j: (i, 0)),
                  pl.BlockSpec((tm, kc), lambda i, j: (i, 0)),
                  pl.BlockSpec((ka, tn), lambda i, j: (0, j)),
                  pl.BlockSpec((kc, tn), lambda i, j: (0, j)),
                  pl.BlockSpec((tm, tn), lambda i, j: (i, o + j)),
                  pl.BlockSpec((tm, tn), lambda i, j: (i, o + nb + j))],
        out_specs=pl.BlockSpec((tm, tn), lambda i, j: (i, j)),
        compiler_params=_cparams("arbitrary", "arbitrary"),
        name="merge_branches",
    )(y_attn, y_conv, wa_bf, wc_bf, p_arr, p_arr)


def _split_bf16(x):
    hi = x.astype(BF16)
    return hi, (x - hi.astype(F32)).astype(BF16)


def _merge2_kernel(y_ref, wm_ref, x_ref, m_ref, gn_ref, wr_ref, br_ref, xo_ref, hp_ref, ids_ref, tw_ref):
    tm = x_ref.shape[0]
    rb = min(tm, MIXOUT_ROW_BLOCK)
    w_hi, w_lo = _split_bf16(wr_ref[...])
    mixes = [jnp.dot(y_ref[r0:r0 + rb, :], wm_ref[...], preferred_element_type=F32) for r0 in range(0, tm, rb)]
    for r0, mix in zip(range(0, tm, rb), mixes):
        rows = slice(r0, r0 + rb)
        xn = x_ref[rows, :] + m_ref[0, 2:3, :] * mix
        xo_ref[rows, :] = xn
        h = _rms(xn) * gn_ref[...] * (1.0 + m_ref[0, 4:5, :]) + m_ref[0, 3:4, :]
        hp_ref[rows, :] = _pack_pairs(h)

        h_hi, h_lo = _split_bf16(h)
        logits = (lax.dot_general(w_hi, h_hi, NT_DIMS, preferred_element_type=F32)
                  + lax.dot_general(w_hi, h_lo, NT_DIMS, preferred_element_type=F32)
                  + lax.dot_general(w_lo, h_hi, NT_DIMS, preferred_element_type=F32)) + br_ref[...]
        n_exp = logits.shape[0]
        expert = lax.broadcasted_iota(jnp.int32, logits.shape, 0)
        vals = []
        for k in range(TOP_K):
            mx = jnp.max(logits, axis=0, keepdims=True)
            idx = jnp.min(jnp.where(logits == mx, expert, n_exp), axis=0, keepdims=True)
            ids_ref[k:k + 1, rows] = idx
            vals.append(mx)
            logits = jnp.where(expert == idx, -jnp.inf, logits)
        es = [jnp.exp(v - vals[0]) for v in vals]
        inv = 1.0 / functools.reduce(jnp.add, es)
        for k in range(TOP_K):
            tw_ref[k:k + 1, rows] = es[k] * inv


def _merge2(y, wm_bf, x2, mods, mod_row, norm_g, w_router_t, b_router, *, seq):
    rows, d = x2.shape
    n_exp = w_router_t.shape[0]
    tm = _tile(seq, 512)
    tiles_per_seq = seq // tm
    return pl.pallas_call(
        _merge2_kernel,
        out_shape=(jax.ShapeDtypeStruct((rows, d), F32),
                   jax.ShapeDtypeStruct((rows, d // 2), jnp.uint32),
                   jax.ShapeDtypeStruct((TOP_K, rows), jnp.int32),
                   jax.ShapeDtypeStruct((TOP_K, rows), F32)),
        grid=(rows // tm,),
        in_specs=[pl.BlockSpec((tm, d), lambda i: (i, 0)),
                  pl.BlockSpec((d, d), lambda i: (0, 0)),
                  pl.BlockSpec((tm, d), lambda i: (i, 0)),
                  pl.BlockSpec((1, N_MOD, d), lambda i: (mod_row(i // tiles_per_seq), 0, 0)),
                  pl.BlockSpec((1, d), lambda i: (0, 0)),
                  pl.BlockSpec((n_exp, d), lambda i: (0, 0)),
                  pl.BlockSpec((n_exp, 1), lambda i: (0, 0))],
        out_specs=(pl.BlockSpec((tm, d), lambda i: (i, 0)),
                   pl.BlockSpec((tm, d // 2), lambda i: (i, 0)),
                   pl.BlockSpec((TOP_K, tm), lambda i: (0, i)),
                   pl.BlockSpec((TOP_K, tm), lambda i: (0, i))),
        compiler_params=_cparams("arbitrary"),
        name="mixout_router",
    )(y, wm_bf, x2, mods, norm_g, w_router_t, b_router)


def _plan_kernel(ids_ref, dest_ref, te_ref, nv_ref, cnt_ref, run_ref, gs_ref, *, n_exp, tm_e):
    p = pl.program_id(0)
    t = pl.program_id(1)
    ids = ids_ref[...]
    tt = ids.shape[1]
    expert = lax.broadcasted_iota(jnp.int32, (n_exp, tt), 0)
    hit = [ids[k:k + 1, :] == expert for k in range(TOP_K)]
    member = functools.reduce(jnp.add, [h.astype(F32) for h in hit])
    tile_count = jnp.sum(member, axis=1, keepdims=True)

    @pl.when((p == 0) & (t == 0))
    def _():
        cnt_ref[...] = jnp.zeros_like(cnt_ref)

    @pl.when(p == 0)
    def _():
        cnt_ref[...] += tile_count

    @pl.when((p == 1) & (t == 0))
    def _():
        cnt = cnt_ref[...]
        ptiles = jnp.floor((cnt + (tm_e - 1)) * (1.0 / tm_e))
        r = lax.broadcasted_iota(jnp.int32, (n_exp, n_exp), 0)
        cidx = lax.broadcasted_iota(jnp.int32, (n_exp, n_exp), 1)
        lower = (cidx < r).astype(BF16)
        p_hi = jnp.floor(ptiles * (1.0 / 16.0))
        p_lo = ptiles - 16.0 * p_hi
        gstart = (16.0 * jnp.dot(lower, p_hi.astype(BF16), preferred_element_type=F32)
                  + jnp.dot(lower, p_lo.astype(BF16), preferred_element_type=F32))
        gs_ref[...] = gstart
        run_ref[...] = jnp.zeros_like(run_ref)
        ntp = te_ref.shape[1]
        tile = lax.broadcasted_iota(jnp.int32, (n_exp, ntp), 1).astype(F32)
        e2 = lax.broadcasted_iota(jnp.int32, (n_exp, ntp), 0).astype(F32)
        gend = gstart[:, 0:1] + ptiles[:, 0:1]
        te = jnp.sum((gend <= tile).astype(F32), axis=0, keepdims=True)
        used = te < n_exp
        te = jnp.minimum(te, n_exp - 1.0)
        onehot = (e2 == te).astype(F32)
        start_e = jnp.sum(onehot * gstart[:, 0:1], axis=0, keepdims=True)
        cnt_e = jnp.sum(onehot * cnt[:, 0:1], axis=0, keepdims=True)
        nv = jnp.clip(cnt_e - (tile[0:1] - start_e) * tm_e, 0.0, float(tm_e))
        te_ref[...] = te.astype(jnp.int32)
        nv_ref[...] = jnp.where(used, nv, 0.0).astype(jnp.int32)

    @pl.when(p == 1)
    def _():
        rr = lax.broadcasted_iota(jnp.int32, (tt, tt), 0)
        cc = lax.broadcasted_iota(jnp.int32, (tt, tt), 1)
        upper = (rr < cc).astype(BF16)
        before = jnp.dot(member.astype(BF16), upper, preferred_element_type=F32)
        pos = gs_ref[:, 0:1] * tm_e + run_ref[:, 0:1] + before
        for k in range(TOP_K):
            d = jnp.sum(jnp.where(hit[k], pos, 0.0), axis=0, keepdims=True)
            dest_ref[k:k + 1, :] = d.astype(jnp.int32)
        run_ref[...] += tile_count


def _plan(ids, *, n_exp, tm_e, n_tiles_pad):
    n_tok = ids.shape[1]
    tt = _tile(n_tok, 512)
    kern = functools.partial(_plan_kernel, n_exp=n_exp, tm_e=tm_e)
    return pl.pallas_call(
        kern,
        out_shape=(jax.ShapeDtypeStruct((TOP_K, n_tok), jnp.int32),
                   jax.ShapeDtypeStruct((1, n_tiles_pad), jnp.int32),
                   jax.ShapeDtypeStruct((1, n_tiles_pad), jnp.int32)),
        grid=(2, n_tok // tt),
        in_specs=[pl.BlockSpec((TOP_K, tt), lambda p, t: (0, t))],
        out_specs=(pl.BlockSpec((TOP_K, tt), lambda p, t: (0, p * t)),
                   pl.BlockSpec((1, n_tiles_pad), lambda p, t: (0, 0)),
                   pl.BlockSpec((1, n_tiles_pad), lambda p, t: (0, 0))),
        scratch_shapes=[pltpu.VMEM((n_exp, LANES), F32)] * 3,
        compiler_params=_cparams("arbitrary", "arbitrary"),
        name="route_plan",
    )(ids)


def _dest_tiles(dest, tt):
    k, n = dest.shape
    return dest.T.reshape(n // tt, 1, tt * k)


def _dispatch_body(dest_ref, h_ref, hg_ref, sem):
    groups = h_ref.shape[0]

    def issue(g, carry):
        for s in range(SUBLANES):
            for k in range(TOP_K):
                d = dest_ref[0, 0, (g * SUBLANES + s) * TOP_K + k]
                pltpu.make_async_copy(h_ref.at[g, pl.ds(s, 1)], hg_ref.at[pl.ds(d, 1)],
                                      sem.at[k % 2]).start(priority=k % 2)
        return carry

    lax.fori_loop(0, groups, issue, 0)
    for p in range(2):
        for _ in range(TOP_K // 2):
            pltpu.make_async_copy(h_ref, h_ref, sem.at[p]).wait()


def _dispatch_first_kernel(dest_ref, h_ref, hg_ref, sem):
    _dispatch_body(dest_ref, h_ref, hg_ref, sem)


def _dispatch_next_kernel(dest_ref, h_ref, hg_in_ref, hg_ref, sem):
    del hg_in_ref
    _dispatch_body(dest_ref, h_ref, hg_ref, sem)


def _dispatch(dest, hp, hg, *, n_rows_pad):
    n_tok, dp = hp.shape
    tt = _tile(n_tok, 1024)
    dest_t = _dest_tiles(dest, tt)
    in_specs = [pl.BlockSpec((1, 1, tt * TOP_K), lambda i: (i, 0, 0), memory_space=pltpu.SMEM),
                pl.BlockSpec((tt // SUBLANES, SUBLANES, dp), lambda i: (i, 0, 0))]
    args = [dest_t, hp.reshape(n_tok // SUBLANES, SUBLANES, dp)]
    aliases = {}
    kern = _dispatch_first_kernel
    if hg is not None:
        in_specs.append(pl.BlockSpec(memory_space=pl.ANY))
        args.append(hg)
        aliases = {2: 0}
        kern = _dispatch_next_kernel
    return pl.pallas_call(
        kern,
        out_shape=jax.ShapeDtypeStruct((n_rows_pad, dp), jnp.uint32),
        grid=(n_tok // tt,),
        in_specs=in_specs,
        out_specs=pl.BlockSpec(memory_space=pl.ANY),
        scratch_shapes=[pltpu.SemaphoreType.DMA((2,))],
        input_output_aliases=aliases,
        compiler_params=pltpu.CompilerParams(dimension_semantics=("arbitrary",),
                                             vmem_limit_bytes=VMEM_LIMIT_BYTES, has_side_effects=True),
        name="moe_dispatch",
    )(*args)


def _mlp_kernel(te_ref, nv_ref, hg_ref, wgu_ref, bgu_ref, wd_ref, bd_ref, o_ref):
    del te_ref
    nv = nv_ref[pl.program_id(0)]

    @pl.when(nv == 0)
    def _():
        o_ref[...] = jnp.zeros_like(o_ref)

    @pl.when(nv > 0)
    def _():
        f = wd_ref.shape[1]
        packed = hg_ref[...]
        row = lax.broadcasted_iota(jnp.int32, packed.shape, 0)
        packed = jnp.where(row < nv, packed, jnp.uint32(0))
        h = _unpack_pairs_f32(packed).astype(BF16)
        gu = jnp.dot(h, wgu_ref[0], preferred_element_type=F32) + bgu_ref[0]
        gate = jnp.minimum(gu[:, :f], SWIGLU_LIMIT)
        lin = jnp.clip(gu[:, f:], -SWIGLU_LIMIT, SWIGLU_LIMIT)
        act = gate * jax.nn.sigmoid(SWIGLU_ALPHA * gate) * (lin + 1.0)
        eo = jnp.dot(act.astype(BF16), wd_ref[0], preferred_element_type=F32) + bd_ref[0]
        o_ref[...] = _pack_pairs(eo)


def _expert_mlp(te, nv, hg, wgu_bf, bgu, wd_bf, bd, *, tm_e, n_tiles):
    n_exp, d, f2 = wgu_bf.shape
    f = wd_bf.shape[1]
    dp = hg.shape[1]
    grid_spec = pltpu.PrefetchScalarGridSpec(
        num_scalar_prefetch=2,
        grid=(n_tiles,),
        in_specs=[pl.BlockSpec((tm_e, dp), lambda i, te, nv: (i, 0)),
                  pl.BlockSpec((1, d, f2), lambda i, te, nv: (te[i], 0, 0)),
                  pl.BlockSpec((1, 1, f2), lambda i, te, nv: (te[i], 0, 0)),
                  pl.BlockSpec((1, f, d), lambda i, te, nv: (te[i], 0, 0)),
                  pl.BlockSpec((1, 1, d), lambda i, te, nv: (te[i], 0, 0))],
        out_specs=pl.BlockSpec((tm_e, dp), lambda i, te, nv: (i, 0)),
    )
    return pl.pallas_call(
        _mlp_kernel,
        out_shape=jax.ShapeDtypeStruct(hg.shape, jnp.uint32),
        grid_spec=grid_spec,
        compiler_params=_cparams("arbitrary"),
        name="expert_mlp",
    )(te, nv, hg, wgu_bf, bgu.reshape(n_exp, 1, f2), wd_bf, bd.reshape(n_exp, 1, d))


def _combine_kernel(dest_ref, dest_next_ref, tw_ref, x_ref, m_ref, eo_ref, o_ref, g_ref, sem):
    tt = x_ref.shape[0]
    groups = tt // SUBLANES
    dp = g_ref.shape[-1]
    i = pl.program_id(0)
    slot = i % 2

    def issue_all(idx_ref, sl):
        def issue(g, carry):
            for s in range(SUBLANES):
                for k in range(TOP_K):
                    d = idx_ref[0, 0, (g * SUBLANES + s) * TOP_K + k]
                    pltpu.make_async_copy(eo_ref.at[pl.ds(d, 1)], g_ref.at[sl, k, g, pl.ds(s, 1)],
                                          sem.at[sl]).start(priority=k % 2)
            return carry
        lax.fori_loop(0, groups, issue, 0)

    @pl.when(i == 0)
    def _():
        issue_all(dest_ref, 0)

    @pl.when(i + 1 < pl.num_programs(0))
    def _():
        issue_all(dest_next_ref, 1 - slot)

    pltpu.make_async_copy(g_ref.at[slot], g_ref.at[slot], sem.at[slot]).wait()
    y = None
    for k in range(TOP_K):
        part = tw_ref[:, k:k + 1] * _unpack_pairs_f32(g_ref[slot, k].reshape(tt, dp))
        y = part if y is None else y + part
    o_ref[...] = x_ref[...] + m_ref[0, 5:6, :] * y


def _combine(dest, tw_t, x2, mods, mod_row, eo, *, seq):
    n_tok, d = x2.shape
    dp = eo.shape[1]
    tt = _tile(seq, 256)
    tiles_per_seq = seq // tt
    dest_t = _dest_tiles(dest, tt)
    n_steps = n_tok // tt
    return pl.pallas_call(
        _combine_kernel,
        out_shape=jax.ShapeDtypeStruct((n_tok, d), F32),
        grid=(n_steps,),
        in_specs=[pl.BlockSpec((1, 1, tt * TOP_K), lambda i: (i, 0, 0), memory_space=pltpu.SMEM),
                  pl.BlockSpec((1, 1, tt * TOP_K), lambda i: (jnp.minimum(i + 1, n_steps - 1), 0, 0),
                               memory_space=pltpu.SMEM),
                  pl.BlockSpec((tt, TOP_K), lambda i: (i, 0)),
                  pl.BlockSpec((tt, d), lambda i: (i, 0)),
                  pl.BlockSpec((1, N_MOD, d), lambda i: (mod_row(i // tiles_per_seq), 0, 0)),
                  pl.BlockSpec(memory_space=pl.ANY)],
        out_specs=pl.BlockSpec((tt, d), lambda i: (i, 0)),
        scratch_shapes=[pltpu.VMEM((2, TOP_K, tt // SUBLANES, SUBLANES, dp), jnp.uint32),
                        pltpu.SemaphoreType.DMA((2,))],
        compiler_params=_cparams("arbitrary"),
        name="moe_combine",
    )(dest_t, dest_t, tw_t, x2, mods, eo)


def _rope_tables(seq, hd):
    n_freq = hd // 4
    pos = jnp.arange(seq)
    row = (pos // GRID_W).astype(F32)
    col = (pos % GRID_W).astype(F32)
    inv = ROPE_BASE ** (-jnp.arange(n_freq, dtype=F32) / n_freq)
    ang = jnp.concatenate([row[:, None] * inv, col[:, None] * inv], axis=1)
    ang = jnp.concatenate([ang, ang], axis=1)
    sign = jnp.where(jnp.arange(hd) < hd // 2, -1.0, 1.0)
    return jnp.cos(ang), sign * jnp.sin(ang)


def _half_major(a, hd):
    lead = a.shape[:-1]
    a = a.reshape(*lead, a.shape[-1] // hd, 2, 2, hd // 4)
    return jnp.swapaxes(a, -3, -2).reshape(*lead, -1)


def kernel(x, c, ctx, c_ctx, w_ada, b_ada, norm_mix_g, norm_ffn_g, w_in, q_norm_g, k_norm_g, lambda_qk,
           subln_g, conv_w, w_attn_out, w_conv_out, w_mix_out, w_router, b_router, w_gate_up, b_gate_up,
           w_down, b_down):
    batch, seq, d = x.shape
    n_ctx = ctx.shape[1]
    depth = w_ada.shape[0]
    hd = q_norm_g.shape[-1]
    n_cols = w_in.shape[-1]
    qkw = (n_cols - 5 * d) // 3
    n_heads = qkw // (2 * hd)
    n_exp = w_router.shape[-1]
    u_off = 3 * qkw
    ga_off = u_off + 3 * d

    cos, sin_s = _rope_tables(seq, hd)
    n_mod_rows = -(-(batch + 1) // 8) * 8
    cv = jnp.zeros((n_mod_rows, d), F32).at[:batch].set(c).at[batch].set(c_ctx)
    mods_all = _adaln(cv, w_ada, b_ada).reshape(depth, n_mod_rows, N_MOD, d)
    lat_row = lambda b: b
    ctx_row = lambda b: batch

    xs = x.reshape(batch * seq, d)
    cs = ctx.reshape(batch * n_ctx, d)
    for l in range(depth):
        last = l == depth - 1
        lam_init = 0.8 - 0.6 * math.exp(-0.3 * l)
        mods = mods_all[l]
        g_mix = norm_mix_g[l][None]
        g_ffn = norm_ffn_g[l][None]
        q_g = _half_major(q_norm_g[l][None], hd)
        k_g = _half_major(k_norm_g[l][None], hd)
        sub_g = subln_g[l][None]
        w_in_bf = _layer_bf16(w_in, l) if l == 0 else w_in_next
        w_in_bf = w_in_bf.at[:, :2 * qkw].set(_half_major(w_in_bf[:, :2 * qkw], hd))

        p_lat = _inproj(xs, mods, lat_row, g_mix, w_in_bf, q_g, k_g, cos, sin_s,
                        qkw=qkw, n_cols=n_cols, seq=seq, rope=True, hd=hd)
        p_ctx = _inproj(cs, mods, ctx_row, g_mix, w_in_bf, q_g, k_g, cos, sin_s,
                        qkw=qkw, n_cols=(3 * qkw if last else n_cols), seq=batch * n_ctx, rope=False, hd=hd)
        attn_args = dict(batch=batch, n_heads=n_heads, hd=hd, qkw=qkw, lam_init=lam_init)
        ride = [(w_gate_up, l), (w_down, l), (w_attn_out, l), (w_conv_out, l), (w_mix_out, l)]
        if not last:
            ride.append((w_in, l + 1))
        y_attn, wgu_bf, wd_bf, wa_bf, wc_bf, wm_bf, *rest = _attention(
            lambda_qk[l], sub_g, p_lat, [p_ctx, p_lat], q_len=seq, kv_lens=[n_ctx, seq], ride=ride, **attn_args)
        if not last:
            w_in_next, = rest
        y_conv = _short_conv(p_lat, conv_w[l], batch=batch, seq=seq, u_off=u_off, width=d)
        w_router_t = w_router[l].T
        b_router_c = b_router[l][:, None]
        y_lat = _merge1(y_attn, y_conv, wa_bf, wc_bf, p_lat, ga_off=ga_off)
        xs, hp_lat, ids_lat, tw_lat = _merge2(y_lat, wm_bf, xs, mods, lat_row, g_ffn, w_router_t, b_router_c,
                                              seq=seq)
        if not last:
            y_attn_c, = _attention(lambda_qk[l], sub_g, p_ctx, [p_ctx], q_len=n_ctx, kv_lens=[n_ctx],
                                   **attn_args)
            y_conv_c = _short_conv(p_ctx, conv_w[l], batch=batch, seq=n_ctx, u_off=u_off, width=d)
            y_ctx = _merge1(y_attn_c, y_conv_c, wa_bf, wc_bf, p_ctx, ga_off=ga_off)
            cs, hp_ctx, ids_ctx, tw_ctx = _merge2(y_ctx, wm_bf, cs, mods, ctx_row, g_ffn, w_router_t,
                                                  b_router_c, seq=batch * n_ctx)
            ids = jnp.concatenate([ids_ctx, ids_lat], axis=1)
        else:
            ids = ids_lat

        n_tok = ids.shape[1]
        n_pairs = n_tok * TOP_K
        tm_e = 512 if n_pairs >= 32768 else 32
        n_tiles = n_pairs // tm_e + n_exp
        n_tiles_pad = -(-n_tiles // LANES) * LANES
        dest, te, nv = _plan(ids, n_exp=n_exp, tm_e=tm_e, n_tiles_pad=n_tiles_pad)
        n_rows_pad = n_tiles * tm_e
        if not last:
            n_c = batch * n_ctx
            hg = _dispatch(dest[:, :n_c], hp_ctx, None, n_rows_pad=n_rows_pad)
            hg = _dispatch(dest[:, n_c:], hp_lat, hg, n_rows_pad=n_rows_pad)
        else:
            hg = _dispatch(dest, hp_lat, None, n_rows_pad=n_rows_pad)
        eo = _expert_mlp(te[0], nv[0], hg, wgu_bf, b_gate_up[l], wd_bf, b_down[l], tm_e=tm_e, n_tiles=n_tiles)
        if not last:
            cs = _combine(dest[:, :n_c], tw_ctx.T, cs, mods, ctx_row, eo, seq=n_ctx)
            xs = _combine(dest[:, n_c:], tw_lat.T, xs, mods, lat_row, eo, seq=seq)
        else:
            xs = _combine(dest, tw_lat.T, xs, mods, lat_row, eo, seq=seq)
    return xs.reshape(batch, seq, d)
```

```python
import functools
import math

import jax
import jax.numpy as jnp
from jax import lax
from jax.experimental import pallas as pl
from jax.experimental.pallas import tpu as pltpu

GRID_W = 64
TOP_K = 4
ROPE_BASE = 10000.0
RMS_EPS = 1e-6
SWIGLU_LIMIT = 7.0
SWIGLU_ALPHA = 1.702
N_MOD = 6
LANES = 128
SUBLANES = 8
CAST_BLOCK_ELEMS = 2 ** 21
INPROJ_ROW_BLOCK = 256
MERGE_ROW_BLOCK = 256
MIXOUT_ROW_BLOCK = 128
ATTN_ROW_BLOCK = 128
ATTN_KEY_CHUNK = 256
ATTN_SCORE_SLOTS = 16
VMEM_LIMIT_BYTES = 56 * 2 ** 20

F32 = jnp.float32
BF16 = jnp.bfloat16
NT_DIMS = (((1,), (1,)), ((), ()))


def _cparams(*sem):
    return pltpu.CompilerParams(dimension_semantics=sem, vmem_limit_bytes=VMEM_LIMIT_BYTES)


def _tile(n, pref):
    if n <= pref:
        return n
    for t in range(pref, 7, -1):
        if n % t == 0 and t % 8 == 0:
            return t
    return n


def _pack_pairs(x):
    n = x.shape[1] // 2
    hi = lax.bitcast_convert_type(x[:, :n].astype(BF16).astype(F32), jnp.uint32)
    lo = lax.bitcast_convert_type(x[:, n:].astype(BF16).astype(F32), jnp.uint32)
    return hi | (lo >> 16)


def _unpack_pairs_f32(p):
    left = lax.bitcast_convert_type(p & jnp.uint32(0xFFFF0000), F32)
    right = lax.bitcast_convert_type(p << 16, F32)
    return jnp.concatenate([left, right], axis=1)


def _rms(x):
    return x * lax.rsqrt(jnp.mean(x * x, axis=-1, keepdims=True) + RMS_EPS)


def _cast_kernel(w_ref, o_ref):
    o_ref[...] = w_ref[0].astype(BF16)


def _layer_bf16(w, layer):
    cols = w.shape[-1]
    w3 = w.reshape(w.shape[0], -1, cols)
    rows = w3.shape[1]
    tr = _tile(rows, max(8, CAST_BLOCK_ELEMS // cols))
    out = pl.pallas_call(
        _cast_kernel,
        out_shape=jax.ShapeDtypeStruct((rows, cols), BF16),
        grid=(rows // tr,),
        in_specs=[pl.BlockSpec((1, tr, cols), lambda i: (layer, i, 0))],
        out_specs=pl.BlockSpec((tr, cols), lambda i: (i, 0)),
        compiler_params=_cparams("arbitrary"),
        name="cast_bf16",
    )(w3)
    return out.reshape(w.shape[1:])


def _adaln_kernel(cv_ref, w_ref, b_ref, o_ref):
    cv = cv_ref[...]
    a = (cv * jax.nn.sigmoid(cv)).astype(BF16)
    o_ref[0] = jnp.dot(a, w_ref[0].astype(BF16), preferred_element_type=F32) + b_ref[0]


def _adaln(cv, w_ada, b_ada):
    n_layers, d, n = w_ada.shape
    r = cv.shape[0]
    tn = _tile(n, 1024)
    return pl.pallas_call(
        _adaln_kernel,
        out_shape=jax.ShapeDtypeStruct((n_layers, r, n), F32),
        grid=(n_layers, n // tn),
        in_specs=[pl.BlockSpec((r, d), lambda l, j: (0, 0)),
                  pl.BlockSpec((1, d, tn), lambda l, j: (l, 0, j)),
                  pl.BlockSpec((1, 1, tn), lambda l, j: (l, 0, j))],
        out_specs=pl.BlockSpec((1, r, tn), lambda l, j: (l, 0, j)),
        compiler_params=_cparams("arbitrary", "arbitrary"),
        name="adaln",
    )(cv, w_ada, b_ada.reshape(n_layers, 1, n))


def _inproj_kernel(x_ref, m_ref, g_ref, w_ref, qg_ref, kg_ref, cos_ref, sin_ref, o_ref, h_ref,
                   *, nq, rope, hd, q_scale):
    j = pl.program_id(1)

    tm, tn = o_ref.shape
    rb = min(tm, INPROJ_ROW_BLOCK)

    def qk_step(gain, first):
        if first:
            for r0 in range(0, tm, rb):
                h = _rms(x_ref[r0:r0 + rb, :]) * g_ref[...] * (1.0 + m_ref[0, 1:2, :]) + m_ref[0, 0:1, :]
                h_ref[r0:r0 + rb, :] = h.astype(BF16)
        accs = [jnp.dot(h_ref[r0:r0 + rb, :], w_ref[...], preferred_element_type=F32) for r0 in range(0, tm, rb)]
        for r0, acc in zip(range(0, tm, rb), accs):
            for g in range(0, tn, hd):
                n = _rms(acc[:, g:g + hd]) * gain
                if rope:
                    n = (n * cos_ref[r0:r0 + rb, :]
                         + pltpu.roll(n, hd // 2, axis=1) * sin_ref[r0:r0 + rb, :])
                o_ref[r0:r0 + rb, g:g + hd] = n.astype(BF16)

    @pl.when(j == 0)
    def _():
        qk_step(qg_ref[...] * q_scale, True)

    @pl.when((j > 0) & (j < nq))
    def _():
        qk_step(qg_ref[...] * q_scale, False)

    @pl.when((j >= nq) & (j < 2 * nq))
    def _():
        qk_step(kg_ref[...], False)

    @pl.when(j >= 2 * nq)
    def _():
        o_ref[...] = jnp.dot(h_ref[...], w_ref[...], preferred_element_type=F32).astype(BF16)


def _inproj(x2, mods, mod_row, norm_g, w_bf, q_g, k_g, cos, sin_s, *, qkw, n_cols, seq, rope, hd):
    rows, d = x2.shape
    tm = _tile(seq, 1024)
    tn = _tile(qkw, 1024)
    tiles_per_seq = seq // tm
    q_scale = float(hd) ** -0.5 * math.log2(math.e)
    kern = functools.partial(_inproj_kernel, nq=qkw // tn, rope=rope, hd=hd, q_scale=q_scale)
    return pl.pallas_call(
        kern,
        out_shape=jax.ShapeDtypeStruct((rows, n_cols), BF16),
        grid=(rows // tm, n_cols // tn),
        in_specs=[pl.BlockSpec((tm, d), lambda i, j: (i, 0)),
                  pl.BlockSpec((1, N_MOD, d), lambda i, j: (mod_row(i // tiles_per_seq), 0, 0)),
                  pl.BlockSpec((1, d), lambda i, j: (0, 0)),
                  pl.BlockSpec((d, tn), lambda i, j: (0, j)),
                  pl.BlockSpec((1, hd), lambda i, j: (0, 0)),
                  pl.BlockSpec((1, hd), lambda i, j: (0, 0)),
                  pl.BlockSpec((tm, hd), lambda i, j: ((i % tiles_per_seq) if rope else 0, 0)),
                  pl.BlockSpec((tm, hd), lambda i, j: ((i % tiles_per_seq) if rope else 0, 0))],
        out_specs=pl.BlockSpec((tm, tn), lambda i, j: (i, j)),
        scratch_shapes=[pltpu.VMEM((tm, d), BF16)],
        compiler_params=_cparams("arbitrary", "arbitrary"),
        name="inproj",
    )(x2, mods, norm_g, w_bf, q_g, k_g, cos, sin_s)


def _attn_kernel(lq_ref, sg_ref, q_ref, *refs, n_src, n_ride, lam_init, hd):
    k_refs = refs[0:2 * n_src:2]
    v_refs = refs[1:2 * n_src:2]
    ride_in = refs[2 * n_src:2 * n_src + n_ride]
    o_ref = refs[2 * n_src + n_ride]
    ride_out = refs[2 * n_src + n_ride + 1:2 * n_src + 2 * n_ride + 1]
    s_ref = refs[-1]
    for w_in_ref, w_out_ref in zip(ride_in, ride_out):
        w_out_ref[...] = w_in_ref[0].astype(BF16)
    lq = lq_ref[...]
    lam = (jnp.exp(jnp.sum(lq[0:1] * lq[1:2], axis=1, keepdims=True))
           - jnp.exp(jnp.sum(lq[2:3] * lq[3:4], axis=1, keepdims=True)) + lam_init)
    tq = q_ref.shape[0]
    rb = min(tq, ATTN_ROW_BLOCK)
    chunks = []
    col = 0
    for si in range(n_src):
        n = k_refs[si].shape[0]
        for st in range(0, n, ATTN_KEY_CHUNK):
            size = min(ATTN_KEY_CHUNK, n - st)
            chunks.append((si, st, size, col))
            col += size

    def lane_tiles(size):
        step = LANES if size % LANES == 0 else size
        return [(j, step) for j in range(0, size, step)]

    class RowReduce:
        def __init__(self, op, combine):
            self.op, self.combine, self.wide, self.narrow = op, combine, None, None

        def add(self, part):
            if part.shape[1] == LANES:
                self.wide = part if self.wide is None else self.combine(self.wide, part)
            else:
                part = self.op(part, axis=1, keepdims=True)
                self.narrow = part if self.narrow is None else self.combine(self.narrow, part)

        def result(self):
            out = [] if self.narrow is None else [self.narrow]
            if self.wide is not None:
                out.append(self.op(self.wide, axis=1, keepdims=True))
            return functools.reduce(self.combine, out)

    for r0 in range(0, tq, rb):
        maps = []
        for m in range(2):
            qm = q_ref[r0:r0 + rb, m * hd:(m + 1) * hd]
            slot = (2 * (r0 // rb) + m) % s_ref.shape[0]
            row_max = RowReduce(jnp.max, jnp.maximum)
            for si, st, size, col in chunks:
                s = lax.dot_general(qm, k_refs[si][st:st + size, m * hd:(m + 1) * hd], NT_DIMS,
                                    preferred_element_type=F32)
                s_ref[slot, :, col:col + size] = s
                for j, w in lane_tiles(size):
                    row_max.add(s[:, j:j + w])
            mx = row_max.result()
            mx_wide = jnp.broadcast_to(mx, (rb, LANES))
            row_sum = RowReduce(jnp.sum, jnp.add)
            pv = None
            for si, st, size, col in chunks:
                probs = []
                for j, w in lane_tiles(size):
                    e = jnp.exp2(s_ref[slot, :, col + j:col + j + w] - mx_wide[:, :w])
                    row_sum.add(e)
                    probs.append(e.astype(BF16))
                p = probs[0] if len(probs) == 1 else jnp.concatenate(probs, axis=1)
                d = jnp.dot(p, v_refs[si][st:st + size, :], preferred_element_type=F32)
                pv = d if pv is None else pv + d
            norm = row_sum.result()
            maps.append(pv * (1.0 / norm))
        o = maps[0] - lam * maps[1]
        o = _rms(o) * sg_ref[...] * (1.0 - lam_init)
        o_ref[r0:r0 + rb, :] = o.astype(BF16)


def _attention(lq, sub_g, q_arr, kv_arrs, *, batch, q_len, kv_lens, n_heads, hd, qkw, lam_init, ride=()):
    vd = 2 * hd
    tq = _tile(q_len, 2048)
    rb = min(tq, ATTN_ROW_BLOCK)
    nqb = q_len // tq
    k_off = qkw // vd
    v_off = 2 * qkw // vd
    in_specs = [pl.BlockSpec((4, hd), lambda b, h, i: (0, 0)),
                pl.BlockSpec((1, vd), lambda b, h, i: (0, 0)),
                pl.BlockSpec((tq, vd), lambda b, h, i: (b * nqb + i, h))]
    args = [lq, sub_g, q_arr]
    for arr, n in zip(kv_arrs, kv_lens):
        in_specs.append(pl.BlockSpec((n, vd), lambda b, h, i: (b, k_off + h)))
        in_specs.append(pl.BlockSpec((n, vd), lambda b, h, i: (b, v_off + h)))
        args += [arr, arr]
    n_steps = batch * n_heads * nqb
    out_shape = [jax.ShapeDtypeStruct((batch * q_len, n_heads * vd), BF16)]
    out_specs = [pl.BlockSpec((tq, vd), lambda b, h, i: (b * nqb + i, h))]
    for w, layer in ride:
        cols = w.shape[-1]
        w3 = w.reshape(w.shape[0], -1, cols)
        tr = w3.shape[1] // n_steps
        assert tr * n_steps == w3.shape[1] and tr % 8 == 0, (w.shape, n_steps)
        step = lambda b, h, i: (b * n_heads + h) * nqb + i
        in_specs.append(pl.BlockSpec((1, tr, cols), lambda b, h, i, layer=layer: (layer, step(b, h, i), 0)))
        args.append(w3)
        out_shape.append(jax.ShapeDtypeStruct((w3.shape[1], cols), BF16))
        out_specs.append(pl.BlockSpec((tr, cols), lambda b, h, i: (step(b, h, i), 0)))
    kern = functools.partial(_attn_kernel, n_src=len(kv_arrs), n_ride=len(ride), lam_init=lam_init, hd=hd)
    outs = pl.pallas_call(
        kern,
        out_shape=out_shape,
        grid=(batch, n_heads, nqb),
        in_specs=in_specs,
        out_specs=out_specs,
        scratch_shapes=[pltpu.VMEM((min(ATTN_SCORE_SLOTS, 2 * tq // rb), rb, sum(kv_lens)), F32)],
        compiler_params=_cparams("arbitrary", "arbitrary", "arbitrary"),
        name="diff_attn",
    )(*args)
    return [outs[0]] + [o.reshape(w.shape[1:]) for o, (w, _) in zip(outs[1:], ride)]


def _conv_kernel(u_ref, b_ref, c_ref, w_ref, o_ref):
    cu = c_ref[...].astype(F32) * u_ref[...].astype(F32)
    n = cu.shape[0]
    row = lax.broadcasted_iota(jnp.int32, cu.shape, 0)
    prev = jnp.where(row == 0, 0.0, pltpu.roll(cu, 1, axis=0))
    nxt = jnp.where(row == n - 1, 0.0, pltpu.roll(cu, n - 1, axis=0))
    w = w_ref[...]
    y = b_ref[...].astype(F32) * (w[0:1] * prev + w[1:2] * cu + w[2:3] * nxt)
    o_ref[...] = y.astype(BF16)


def _short_conv(p_arr, conv_w, *, batch, seq, u_off, width):
    tc = _tile(width, 512)
    o = u_off // tc
    nb = width // tc
    return pl.pallas_call(
        _conv_kernel,
        out_shape=jax.ShapeDtypeStruct((batch * seq, width), BF16),
        grid=(batch, nb),
        in_specs=[pl.BlockSpec((seq, tc), lambda b, j: (b, o + j)),
                  pl.BlockSpec((seq, tc), lambda b, j: (b, o + nb + j)),
                  pl.BlockSpec((seq, tc), lambda b, j: (b, o + 2 * nb + j)),
                  pl.BlockSpec((conv_w.shape[0], tc), lambda b, j: (0, j))],
        out_specs=pl.BlockSpec((seq, tc), lambda b, j: (b, j)),
        compiler_params=_cparams("arbitrary", "arbitrary"),
        name="short_conv",
    )(p_arr, p_arr, p_arr, conv_w)


def _merge1_kernel(ya_ref, yc_ref, wa_ref, wc_ref, ga_ref, gb_ref, o_ref):
    tm = o_ref.shape[0]
    rb = min(tm, MERGE_ROW_BLOCK)
    blocks = range(0, tm, rb)
    prods = [(jnp.dot(ya_ref[r0:r0 + rb, :], wa_ref[...], preferred_element_type=F32),
              jnp.dot(yc_ref[r0:r0 + rb, :], wc_ref[...], preferred_element_type=F32)) for r0 in blocks]
    for r0, (a, c) in zip(blocks, prods):
        rows = slice(r0, r0 + rb)
        y = (jax.nn.sigmoid(ga_ref[rows, :].astype(F32)) * a
             + jax.nn.sigmoid(gb_ref[rows, :].astype(F32)) * c)
        o_ref[rows, :] = y.astype(BF16)


def _merge1(y_attn, y_conv, wa_bf, wc_bf, p_arr, *, ga_off):
    rows, ka = y_attn.shape
    kc = y_conv.shape[1]
    d = wa_bf.shape[1]
    tm = _tile(rows, 1024)
    tn = _tile(d, 512)
    o = ga_off // tn
    nb = d // tn
    return pl.pallas_call(
        _merge1_kernel,
        out_shape=jax.ShapeDtypeStruct((rows, d), BF16),
        grid=(rows // tm, nb),
        in_specs=[pl.BlockSpec((tm, ka), lambda i, j: (i, 0)),
                  pl.BlockSpec((tm, kc), lambda i, j: (i, 0)),
                  pl.BlockSpec((ka, tn), lambda i, j: (0, j)),
                  pl.BlockSpec((kc, tn), lambda i, j: (0, j)),
                  pl.BlockSpec((tm, tn), lambda i, j: (i, o + j)),
                  pl.BlockSpec((tm, tn), lambda i, j: (i, o + nb + j))],
        out_specs=pl.BlockSpec((tm, tn), lambda i, j: (i, j)),
        compiler_params=_cparams("arbitrary", "arbitrary"),
        name="merge_branches",
    )(y_attn, y_conv, wa_bf, wc_bf, p_arr, p_arr)


def _split_bf16(x):
    hi = x.astype(BF16)
    return hi, (x - hi.astype(F32)).astype(BF16)


def _merge2_kernel(y_ref, wm_ref, x_ref, m_ref, gn_ref, wr_ref, br_ref, xo_ref, hp_ref, ids_ref, tw_ref):
    tm = x_ref.shape[0]
    rb = min(tm, MIXOUT_ROW_BLOCK)
    w_hi, w_lo = _split_bf16(wr_ref[...])
    mixes = [jnp.dot(y_ref[r0:r0 + rb, :], wm_ref[...], preferred_element_type=F32) for r0 in range(0, tm, rb)]
    for r0, mix in zip(range(0, tm, rb), mixes):
        rows = slice(r0, r0 + rb)
        xn = x_ref[rows, :] + m_ref[0, 2:3, :] * mix
        xo_ref[rows, :] = xn
        h = _rms(xn) * gn_ref[...] * (1.0 + m_ref[0, 4:5, :]) + m_ref[0, 3:4, :]
        hp_ref[rows, :] = _pack_pairs(h)

        h_hi, h_lo = _split_bf16(h)
        logits = (lax.dot_general(w_hi, h_hi, NT_DIMS, preferred_element_type=F32)
                  + lax.dot_general(w_hi, h_lo, NT_DIMS, preferred_element_type=F32)
                  + lax.dot_general(w_lo, h_hi, NT_DIMS, preferred_element_type=F32)) + br_ref[...]
        n_exp = logits.shape[0]
        expert = lax.broadcasted_iota(jnp.int32, logits.shape, 0)
        vals = []
        for k in range(TOP_K):
            mx = jnp.max(logits, axis=0, keepdims=True)
            idx = jnp.min(jnp.where(logits == mx, expert, n_exp), axis=0, keepdims=True)
            ids_ref[k:k + 1, rows] = idx
            vals.append(mx)
            logits = jnp.where(expert == idx, -jnp.inf, logits)
        es = [jnp.exp(v - vals[0]) for v in vals]
        inv = 1.0 / functools.reduce(jnp.add, es)
        for k in range(TOP_K):
            tw_ref[k:k + 1, rows] = es[k] * inv


def _merge2(y, wm_bf, x2, mods, mod_row, norm_g, w_router_t, b_router, *, seq):
    rows, d = x2.shape
    n_exp = w_router_t.shape[0]
    tm = _tile(seq, 512)
    tiles_per_seq = seq // tm
    return pl.pallas_call(
        _merge2_kernel,
        out_shape=(jax.ShapeDtypeStruct((rows, d), F32),
                   jax.ShapeDtypeStruct((rows, d // 2), jnp.uint32),
                   jax.ShapeDtypeStruct((TOP_K, rows), jnp.int32),
                   jax.ShapeDtypeStruct((TOP_K, rows), F32)),
        grid=(rows // tm,),
        in_specs=[pl.BlockSpec((tm, d), lambda i: (i, 0)),
                  pl.BlockSpec((d, d), lambda i: (0, 0)),
                  pl.BlockSpec((tm, d), lambda i: (i, 0)),
                  pl.BlockSpec((1, N_MOD, d), lambda i: (mod_row(i // tiles_per_seq), 0, 0)),
                  pl.BlockSpec((1, d), lambda i: (0, 0)),
                  pl.BlockSpec((n_exp, d), lambda i: (0, 0)),
                  pl.BlockSpec((n_exp, 1), lambda i: (0, 0))],
        out_specs=(pl.BlockSpec((tm, d), lambda i: (i, 0)),
                   pl.BlockSpec((tm, d // 2), lambda i: (i, 0)),
                   pl.BlockSpec((TOP_K, tm), lambda i: (0, i)),
                   pl.BlockSpec((TOP_K, tm), lambda i: (0, i))),
        compiler_params=_cparams("arbitrary"),
        name="mixout_router",
    )(y, wm_bf, x2, mods, norm_g, w_router_t, b_router)


def _plan_kernel(ids_ref, dest_ref, te_ref, nv_ref, cnt_ref, run_ref, gs_ref, *, n_exp, tm_e):
    p = pl.program_id(0)
    t = pl.program_id(1)
    ids = ids_ref[...]
    tt = ids.shape[1]
    expert = lax.broadcasted_iota(jnp.int32, (n_exp, tt), 0)
    hit = [ids[k:k + 1, :] == expert for k in range(TOP_K)]
    member = functools.reduce(jnp.add, [h.astype(F32) for h in hit])
    tile_count = jnp.sum(member, axis=1, keepdims=True)

    @pl.when((p == 0) & (t == 0))
    def _():
        cnt_ref[...] = jnp.zeros_like(cnt_ref)

    @pl.when(p == 0)
    def _():
        cnt_ref[...] += tile_count

    @pl.when((p == 1) & (t == 0))
    def _():
        cnt = cnt_ref[...]
        ptiles = jnp.floor((cnt + (tm_e - 1)) * (1.0 / tm_e))
        r = lax.broadcasted_iota(jnp.int32, (n_exp, n_exp), 0)
        cidx = lax.broadcasted_iota(jnp.int32, (n_exp, n_exp), 1)
        lower = (cidx < r).astype(BF16)
        p_hi = jnp.floor(ptiles * (1.0 / 16.0))
        p_lo = ptiles - 16.0 * p_hi
        gstart = (16.0 * jnp.dot(lower, p_hi.astype(BF16), preferred_element_type=F32)
                  + jnp.dot(lower, p_lo.astype(BF16), preferred_element_type=F32))
        gs_ref[...] = gstart
        run_ref[...] = jnp.zeros_like(run_ref)
        ntp = te_ref.shape[1]
        tile = lax.broadcasted_iota(jnp.int32, (n_exp, ntp), 1).astype(F32)
        e2 = lax.broadcasted_iota(jnp.int32, (n_exp, ntp), 0).astype(F32)
        gend = gstart[:, 0:1] + ptiles[:, 0:1]
        te = jnp.sum((gend <= tile).astype(F32), axis=0, keepdims=True)
        used = te < n_exp
        te = jnp.minimum(te, n_exp - 1.0)
        onehot = (e2 == te).astype(F32)
        start_e = jnp.sum(onehot * gstart[:, 0:1], axis=0, keepdims=True)
        cnt_e = jnp.sum(onehot * cnt[:, 0:1], axis=0, keepdims=True)
        nv = jnp.clip(cnt_e - (tile[0:1] - start_e) * tm_e, 0.0, float(tm_e))
        te_ref[...] = te.astype(jnp.int32)
        nv_ref[...] = jnp.where(used, nv, 0.0).astype(jnp.int32)

    @pl.when(p == 1)
    def _():
        rr = lax.broadcasted_iota(jnp.int32, (tt, tt), 0)
        cc = lax.broadcasted_iota(jnp.int32, (tt, tt), 1)
        upper = (rr < cc).astype(BF16)
        before = jnp.dot(member.astype(BF16), upper, preferred_element_type=F32)
        pos = gs_ref[:, 0:1] * tm_e + run_ref[:, 0:1] + before
        for k in range(TOP_K):
            d = jnp.sum(jnp.where(hit[k], pos, 0.0), axis=0, keepdims=True)
            dest_ref[k:k + 1, :] = d.astype(jnp.int32)
        run_ref[...] += tile_count


def _plan(ids, *, n_exp, tm_e, n_tiles_pad):
    n_tok = ids.shape[1]
    tt = _tile(n_tok, 512)
    kern = functools.partial(_plan_kernel, n_exp=n_exp, tm_e=tm_e)
    return pl.pallas_call(
        kern,
        out_shape=(jax.ShapeDtypeStruct((TOP_K, n_tok), jnp.int32),
                   jax.ShapeDtypeStruct((1, n_tiles_pad), jnp.int32),
                   jax.ShapeDtypeStruct((1, n_tiles_pad), jnp.int32)),
        grid=(2, n_tok // tt),
        in_specs=[pl.BlockSpec((TOP_K, tt), lambda p, t: (0, t))],
        out_specs=(pl.BlockSpec((TOP_K, tt), lambda p, t: (0, p * t)),
                   pl.BlockSpec((1, n_tiles_pad), lambda p, t: (0, 0)),
                   pl.BlockSpec((1, n_tiles_pad), lambda p, t: (0, 0))),
        scratch_shapes=[pltpu.VMEM((n_exp, LANES), F32)] * 3,
        compiler_params=_cparams("arbitrary", "arbitrary"),
        name="route_plan",
    )(ids)


def _dest_tiles(dest, tt):
    k, n = dest.shape
    return dest.T.reshape(n // tt, 1, tt * k)


def _dispatch_body(dest_ref, h_ref, hg_ref, sem):
    groups = h_ref.shape[0]

    def issue(g, carry):
        for s in range(SUBLANES):
            for k in range(TOP_K):
                d = dest_ref[0, 0, (g * SUBLANES + s) * TOP_K + k]
                pltpu.make_async_copy(h_ref.at[g, pl.ds(s, 1)], hg_ref.at[pl.ds(d, 1)],
                                      sem.at[k % 2]).start(priority=k % 2)
        return carry

    lax.fori_loop(0, groups, issue, 0)
    for p in range(2):
        for _ in range(TOP_K // 2):
            pltpu.make_async_copy(h_ref, h_ref, sem.at[p]).wait()


def _dispatch_first_kernel(dest_ref, h_ref, hg_ref, sem):
    _dispatch_body(dest_ref, h_ref, hg_ref, sem)


def _dispatch_next_kernel(dest_ref, h_ref, hg_in_ref, hg_ref, sem):
    del hg_in_ref
    _dispatch_body(dest_ref, h_ref, hg_ref, sem)


def _dispatch(dest, hp, hg, *, n_rows_pad):
    n_tok, dp = hp.shape
    tt = _tile(n_tok, 1024)
    dest_t = _dest_tiles(dest, tt)
    in_specs = [pl.BlockSpec((1, 1, tt * TOP_K), lambda i: (i, 0, 0), memory_space=pltpu.SMEM),
                pl.BlockSpec((tt // SUBLANES, SUBLANES, dp), lambda i: (i, 0, 0))]
    args = [dest_t, hp.reshape(n_tok // SUBLANES, SUBLANES, dp)]
    aliases = {}
    kern = _dispatch_first_kernel
    if hg is not None:
        in_specs.append(pl.BlockSpec(memory_space=pl.ANY))
        args.append(hg)
        aliases = {2: 0}
        kern = _dispatch_next_kernel
    return pl.pallas_call(
        kern,
        out_shape=jax.ShapeDtypeStruct((n_rows_pad, dp), jnp.uint32),
        grid=(n_tok // tt,),
        in_specs=in_specs,
        out_specs=pl.BlockSpec(memory_space=pl.ANY),
        scratch_shapes=[pltpu.SemaphoreType.DMA((2,))],
        input_output_aliases=aliases,
        compiler_params=pltpu.CompilerParams(dimension_semantics=("arbitrary",),
                                             vmem_limit_bytes=VMEM_LIMIT_BYTES, has_side_effects=True),
        name="moe_dispatch",
    )(*args)


def _mlp_kernel(te_ref, nv_ref, hg_ref, wgu_ref, bgu_ref, wd_ref, bd_ref, o_ref):
    del te_ref
    nv = nv_ref[pl.program_id(0)]

    @pl.when(nv == 0)
    def _():
        o_ref[...] = jnp.zeros_like(o_ref)

    @pl.when(nv > 0)
    def _():
        f = wd_ref.shape[1]
        packed = hg_ref[...]
        row = lax.broadcasted_iota(jnp.int32, packed.shape, 0)
        packed = jnp.where(row < nv, packed, jnp.uint32(0))
        h = _unpack_pairs_f32(packed).astype(BF16)
        gu = jnp.dot(h, wgu_ref[0], preferred_element_type=F32) + bgu_ref[0]
        gate = jnp.minimum(gu[:, :f], SWIGLU_LIMIT)
        lin = jnp.clip(gu[:, f:], -SWIGLU_LIMIT, SWIGLU_LIMIT)
        act = gate * jax.nn.sigmoid(SWIGLU_ALPHA * gate) * (lin + 1.0)
        eo = jnp.dot(act.astype(BF16), wd_ref[0], preferred_element_type=F32) + bd_ref[0]
        o_ref[...] = _pack_pairs(eo)


def _expert_mlp(te, nv, hg, wgu_bf, bgu, wd_bf, bd, *, tm_e, n_tiles):
    n_exp, d, f2 = wgu_bf.shape
    f = wd_bf.shape[1]
    dp = hg.shape[1]
    grid_spec = pltpu.PrefetchScalarGridSpec(
        num_scalar_prefetch=2,
        grid=(n_tiles,),
        in_specs=[pl.BlockSpec((tm_e, dp), lambda i, te, nv: (i, 0)),
                  pl.BlockSpec((1, d, f2), lambda i, te, nv: (te[i], 0, 0)),
                  pl.BlockSpec((1, 1, f2), lambda i, te, nv: (te[i], 0, 0)),
                  pl.BlockSpec((1, f, d), lambda i, te, nv: (te[i], 0, 0)),
                  pl.BlockSpec((1, 1, d), lambda i, te, nv: (te[i], 0, 0))],
        out_specs=pl.BlockSpec((tm_e, dp), lambda i, te, nv: (i, 0)),
    )
    return pl.pallas_call(
        _mlp_kernel,
        out_shape=jax.ShapeDtypeStruct(hg.shape, jnp.uint32),
        grid_spec=grid_spec,
        compiler_params=_cparams("arbitrary"),
        name="expert_mlp",
    )(te, nv, hg, wgu_bf, bgu.reshape(n_exp, 1, f2), wd_bf, bd.reshape(n_exp, 1, d))


def _combine_kernel(dest_ref, dest_next_ref, tw_ref, x_ref, m_ref, eo_ref, o_ref, g_ref, sem):
    tt = x_ref.shape[0]
    groups = tt // SUBLANES
    dp = g_ref.shape[-1]
    i = pl.program_id(0)
    slot = i % 2

    def issue_all(idx_ref, sl):
        def issue(g, carry):
            for s in range(SUBLANES):
                for k in range(TOP_K):
                    d = idx_ref[0, 0, (g * SUBLANES + s) * TOP_K + k]
                    pltpu.make_async_copy(eo_ref.at[pl.ds(d, 1)], g_ref.at[sl, k, g, pl.ds(s, 1)],
                                          sem.at[sl]).start(priority=k % 2)
            return carry
        lax.fori_loop(0, groups, issue, 0)

    @pl.when(i == 0)
    def _():
        issue_all(dest_ref, 0)

    @pl.when(i + 1 < pl.num_programs(0))
    def _():
        issue_all(dest_next_ref, 1 - slot)

    pltpu.make_async_copy(g_ref.at[slot], g_ref.at[slot], sem.at[slot]).wait()
    y = None
    for k in range(TOP_K):
        part = tw_ref[:, k:k + 1] * _unpack_pairs_f32(g_ref[slot, k].reshape(tt, dp))
        y = part if y is None else y + part
    o_ref[...] = x_ref[...] + m_ref[0, 5:6, :] * y


def _combine(dest, tw_t, x2, mods, mod_row, eo, *, seq):
    n_tok, d = x2.shape
    dp = eo.shape[1]
    tt = _tile(seq, 256)
    tiles_per_seq = seq // tt
    dest_t = _dest_tiles(dest, tt)
    n_steps = n_tok // tt
    return pl.pallas_call(
        _combine_kernel,
        out_shape=jax.ShapeDtypeStruct((n_tok, d), F32),
        grid=(n_steps,),
        in_specs=[pl.BlockSpec((1, 1, tt * TOP_K), lambda i: (i, 0, 0), memory_space=pltpu.SMEM),
                  pl.BlockSpec((1, 1, tt * TOP_K), lambda i: (jnp.minimum(i + 1, n_steps - 1), 0, 0),
                               memory_space=pltpu.SMEM),
                  pl.BlockSpec((tt, TOP_K), lambda i: (i, 0)),
                  pl.BlockSpec((tt, d), lambda i: (i, 0)),
                  pl.BlockSpec((1, N_MOD, d), lambda i: (mod_row(i // tiles_per_seq), 0, 0)),
                  pl.BlockSpec(memory_space=pl.ANY)],
        out_specs=pl.BlockSpec((tt, d), lambda i: (i, 0)),
        scratch_shapes=[pltpu.VMEM((2, TOP_K, tt // SUBLANES, SUBLANES, dp), jnp.uint32),
                        pltpu.SemaphoreType.DMA((2,))],
        compiler_params=_cparams("arbitrary"),
        name="moe_combine",
    )(dest_t, dest_t, tw_t, x2, mods, eo)


def _rope_tables(seq, hd):
    n_freq = hd // 4
    pos = jnp.arange(seq)
    row = (pos // GRID_W).astype(F32)
    col = (pos % GRID_W).astype(F32)
    inv = ROPE_BASE ** (-jnp.arange(n_freq, dtype=F32) / n_freq)
    ang = jnp.concatenate([row[:, None] * inv, col[:, None] * inv], axis=1)
    ang = jnp.concatenate([ang, ang], axis=1)
    sign = jnp.where(jnp.arange(hd) < hd // 2, -1.0, 1.0)
    return jnp.cos(ang), sign * jnp.sin(ang)


def _half_major(a, hd):
    lead = a.shape[:-1]
    a = a.reshape(*lead, a.shape[-1] // hd, 2, 2, hd // 4)
    return jnp.swapaxes(a, -3, -2).reshape(*lead, -1)


def kernel(x, c, ctx, c_ctx, w_ada, b_ada, norm_mix_g, norm_ffn_g, w_in, q_norm_g, k_norm_g, lambda_qk,
           subln_g, conv_w, w_attn_out, w_conv_out, w_mix_out, w_router, b_router, w_gate_up, b_gate_up,
           w_down, b_down):
    batch, seq, d = x.shape
    n_ctx = ctx.shape[1]
    depth = w_ada.shape[0]
    hd = q_norm_g.shape[-1]
    n_cols = w_in.shape[-1]
    qkw = (n_cols - 5 * d) // 3
    n_heads = qkw // (2 * hd)
    n_exp = w_router.shape[-1]
    u_off = 3 * qkw
    ga_off = u_off + 3 * d

    cos, sin_s = _rope_tables(seq, hd)
    n_mod_rows = -(-(batch + 1) // 8) * 8
    cv = jnp.zeros((n_mod_rows, d), F32).at[:batch].set(c).at[batch].set(c_ctx)
    mods_all = _adaln(cv, w_ada, b_ada).reshape(depth, n_mod_rows, N_MOD, d)
    lat_row = lambda b: b
    ctx_row = lambda b: batch

    xs = x.reshape(batch * seq, d)
    cs = ctx.reshape(batch * n_ctx, d)
    for l in range(depth):
        last = l == depth - 1
        lam_init = 0.8 - 0.6 * math.exp(-0.3 * l)
        mods = mods_all[l]
        g_mix = norm_mix_g[l][None]
        g_ffn = norm_ffn_g[l][None]
        q_g = _half_major(q_norm_g[l][None], hd)
        k_g = _half_major(k_norm_g[l][None], hd)
        sub_g = subln_g[l][None]
        w_in_bf = _layer_bf16(w_in, l) if l == 0 else w_in_next
        w_in_bf = w_in_bf.at[:, :2 * qkw].set(_half_major(w_in_bf[:, :2 * qkw], hd))

        p_lat = _inproj(xs, mods, lat_row, g_mix, w_in_bf, q_g, k_g, cos, sin_s,
                        qkw=qkw, n_cols=n_cols, seq=seq, rope=True, hd=hd)
        p_ctx = _inproj(cs, mods, ctx_row, g_mix, w_in_bf, q_g, k_g, cos, sin_s,
                        qkw=qkw, n_cols=(3 * qkw if last else n_cols), seq=batch * n_ctx, rope=False, hd=hd)
        attn_args = dict(batch=batch, n_heads=n_heads, hd=hd, qkw=qkw, lam_init=lam_init)
        ride = [(w_gate_up, l), (w_down, l), (w_attn_out, l), (w_conv_out, l), (w_mix_out, l)]
        if not last:
            ride.append((w_in, l + 1))
        y_attn, wgu_bf, wd_bf, wa_bf, wc_bf, wm_bf, *rest = _attention(
            lambda_qk[l], sub_g, p_lat, [p_ctx, p_lat], q_len=seq, kv_lens=[n_ctx, seq], ride=ride, **attn_args)
        if not last:
            w_in_next, = rest
        y_conv = _short_conv(p_lat, conv_w[l], batch=batch, seq=seq, u_off=u_off, width=d)
        w_router_t = w_router[l].T
        b_router_c = b_router[l][:, None]
        y_lat = _merge1(y_attn, y_conv, wa_bf, wc_bf, p_lat, ga_off=ga_off)
        xs, hp_lat, ids_lat, tw_lat = _merge2(y_lat, wm_bf, xs, mods, lat_row, g_ffn, w_router_t, b_router_c,
                                              seq=seq)
        if not last:
            y_attn_c, = _attention(lambda_qk[l], sub_g, p_ctx, [p_ctx], q_len=n_ctx, kv_lens=[n_ctx],
                                   **attn_args)
            y_conv_c = _short_conv(p_ctx, conv_w[l], batch=batch, seq=n_ctx, u_off=u_off, width=d)
            y_ctx = _merge1(y_attn_c, y_conv_c, wa_bf, wc_bf, p_ctx, ga_off=ga_off)
            cs, hp_ctx, ids_ctx, tw_ctx = _merge2(y_ctx, wm_bf, cs, mods, ctx_row, g_ffn, w_router_t,
                                                  b_router_c, seq=batch * n_ctx)
            ids = jnp.concatenate([ids_ctx, ids_lat], axis=1)
        else:
            ids = ids_lat

        n_tok = ids.shape[1]
        n_pairs = n_tok * TOP_K
        tm_e = 512 if n_pairs >= 32768 else 32
        n_tiles = n_pairs // tm_e + n_exp
        n_tiles_pad = -(-n_tiles // LANES) * LANES
        dest, te, nv = _plan(ids, n_exp=n_exp, tm_e=tm_e, n_tiles_pad=n_tiles_pad)
        n_rows_pad = n_tiles * tm_e
        if not last:
            n_c = batch * n_ctx
            hg = _dispatch(dest[:, :n_c], hp_ctx, None, n_rows_pad=n_rows_pad)
            hg = _dispatch(dest[:, n_c:], hp_lat, hg, n_rows_pad=n_rows_pad)
        else:
            hg = _dispatch(dest, hp_lat, None, n_rows_pad=n_rows_pad)
        eo = _expert_mlp(te[0], nv[0], hg, wgu_bf, b_gate_up[l], wd_bf, b_down[l], tm_e=tm_e, n_tiles=n_tiles)
        if not last:
            cs = _combine(dest[:, :n_c], tw_ctx.T, cs, mods, ctx_row, eo, seq=n_ctx)
            xs = _combine(dest[:, n_c:], tw_lat.T, xs, mods, lat_row, eo, seq=seq)
        else:
            xs = _combine(dest, tw_lat.T, xs, mods, lat_row, eo, seq=seq)
    return xs.reshape(batch, seq, d)
```
